```python
import jax, jax.numpy as jnp
from jax import lax
import numpy as np

D_MODEL = 2048
BATCH = 1
SEQ = 16384
DEPTH = 2

GRID_W = 64
CTX_LEN = 256
N_MIXERS = 2
N_SUB = 3
POOL_WINDOWS = (2, 4, 8, 16)
N_POOL_GROUPS = len(POOL_WINDOWS)
POOL_G = D_MODEL // N_POOL_GROUPS
HGRN_EXPAND = 128
HGRN_HEADS = D_MODEL // HGRN_EXPAND
HGRN_K = HGRN_EXPAND
HGRN_V = D_MODEL // HGRN_HEADS
CHUNK = 64
D_FF = 5632
N_POOL_LAYERS = (DEPTH + 1) // 2
N_HGRN_LAYERS = DEPTH // 2
ALPHA = (2 * DEPTH) ** 0.25
BETA = (8 * DEPTH) ** -0.25
LN_EPS = 1e-5
RMS_EPS = 1e-6

kernel_name = "hybrid_pool_hgrn2_macaron_deepnorm_prefix"


def layer_norm(x, g, b):
    xf = x.astype(jnp.float32)
    mu = jnp.mean(xf, -1, keepdims=True)
    var = jnp.mean(jnp.square(xf - mu), -1, keepdims=True)
    return ((xf - mu) * lax.rsqrt(var + LN_EPS) * g + b).astype(x.dtype)


def mod_in(x, m, s):
    return x * (1 + m[..., s, 1, :][..., None, :]) + m[..., s, 0, :][..., None, :]


def residual(x, y, m, s, g, b):
    return layer_norm(ALPHA * x + m[..., s, 2, :][..., None, :] * y, g, b)


def swiglu(h, w_in, w_out):
    a, u = jnp.split(h @ w_in, 2, axis=-1)
    return (jax.nn.silu(a) * u) @ w_out


def window_mean(x, w, axis):
    n = x.shape[axis]
    cs = jnp.cumsum(x.astype(jnp.float32), axis=axis)
    pad = [(0, 0)] * x.ndim
    pad[axis] = (1, 0)
    cs = jnp.pad(cs, pad)
    idx = jnp.arange(n)
    lo = jnp.clip(idx - w // 2, 0, n)
    hi = jnp.clip(idx - w // 2 + w, 0, n)
    s = jnp.take(cs, hi, axis=axis) - jnp.take(cs, lo, axis=axis)
    shape = [1] * x.ndim
    shape[axis] = n
    cnt = (hi - lo).astype(jnp.float32).reshape(shape)
    return (s / cnt).astype(x.dtype)


def pool_mixer(h, w, scale, grid):
    b, n, d = h.shape
    if grid:
        rows = n // GRID_W
        t = h.reshape(b, rows, GRID_W, d)
    else:
        t = h
    outs = []
    for gi, win in enumerate(POOL_WINDOWS):
        xg = t[..., gi * POOL_G:(gi + 1) * POOL_G]
        if grid:
            m = window_mean(window_mean(xg, win, 1), win, 2)
        else:
            m = window_mean(xg, win, 1)
        outs.append(m - xg)
    p = jnp.stack(outs, axis=-2).reshape(b, n, N_POOL_GROUPS, POOL_G)
    y = jnp.einsum('bngc,gcd->bngd', p, w).reshape(b, n, d)
    return y * scale


def _split_heads(a):
    b, n, _ = a.shape
    return a.reshape(b, n, HGRN_HEADS, -1).transpose(0, 2, 1, 3)


def _flip(a):
    return jnp.flip(a, axis=2)


def _hgrn_gates(a, lb):
    a = _split_heads(a).astype(jnp.float32)
    lb = lb.reshape(HGRN_HEADS, 1, HGRN_K)
    log_f = jnp.logaddexp(jnp.log(lb), jnp.log1p(-lb) + jax.nn.log_sigmoid(a))
    k = (1.0 - lb) * jax.nn.sigmoid(-a)
    return k, log_f


def _hgrn_project(h, w_in, lb_f, lb_b):
    q, i, af, ab, og = jnp.split(h @ w_in, 5, axis=-1)
    q = _split_heads(jax.nn.silu(q)) * (HGRN_K ** -0.5)
    v = _split_heads(i)
    kf, gf = _hgrn_gates(af, lb_f)
    kb, gb = _hgrn_gates(ab, lb_b)
    return q, v, kf, gf, kb, gb, og


def gla_chunk_scan(q, k, v, g, s0):
    b, nh, t, _ = q.shape
    nc = t // CHUNK

    def to_chunks(a):
        return a.reshape(b, nh, nc, CHUNK, a.shape[-1]).transpose(2, 0, 1, 3, 4)

    mask = jnp.tril(jnp.ones((CHUNK, CHUNK), bool))[:, :, None]

    def step(S, inp):
        qc, kc, vc, gc = inp
        G = jnp.cumsum(gc, axis=-2)
        rel = jnp.where(mask, G[:, :, :, None, :] - G[:, :, None, :, :], -jnp.inf)
        A = jnp.einsum('bhtk,bhsk,bhtsk->bhts', qc, kc, jnp.exp(rel))
        o = jnp.einsum('bhts,bhsv->bhtv', A, vc) + jnp.einsum('bhtk,bhkv->bhtv', qc * jnp.exp(G), S)
        G_last = G[:, :, -1:, :]
        S = jnp.exp(G_last[:, :, 0, :])[..., None] * S + jnp.einsum('bhsk,bhsv->bhkv', kc * jnp.exp(G_last - G), vc)
        return S, o

    S, o = lax.scan(step, s0, (to_chunks(q), to_chunks(k), to_chunks(v), to_chunks(g)))
    o = o.transpose(1, 2, 0, 3, 4).reshape(b, nh, t, -1)
    return o, S


def gla_final_state(k, v, g):
    G = jnp.cumsum(g, axis=-2)
    return jnp.einsum('bhsk,bhsv->bhkv', k * jnp.exp(G[:, :, -1:, :] - G), v)


def _hgrn_readout(o, og, norm_g, w_out):
    b, _, n, _ = o.shape
    o = o * lax.rsqrt(jnp.mean(jnp.square(o), -1, keepdims=True) + RMS_EPS) * norm_g
    o = o.transpose(0, 2, 1, 3).reshape(b, n, D_MODEL)
    return (o * jax.nn.silu(og.astype(jnp.float32))).astype(og.dtype) @ w_out


def hgrn_mixer(h, hc, w_in, lb_f, lb_b, norm_g, w_out, ctx_out):
    b = h.shape[0]
    zero = jnp.zeros((b, HGRN_HEADS, HGRN_K, HGRN_V), jnp.float32)
    if ctx_out:
        qc, vc, kfc, gfc, kbc, gbc, ogc = _hgrn_project(hc, w_in, lb_f, lb_b)
        oc_f, s_f = gla_chunk_scan(qc, kfc, vc, gfc, zero)
        oc_b, s_b = gla_chunk_scan(_flip(qc), _flip(kbc), _flip(vc), _flip(gbc), zero)
        yc = _hgrn_readout(oc_f + _flip(oc_b), ogc, norm_g, w_out)
    else:
        ic, afc, abc = jnp.split(hc @ w_in[:, D_MODEL:4 * D_MODEL], 3, axis=-1)
        vc = _split_heads(ic)
        kfc, gfc = _hgrn_gates(afc, lb_f)
        kbc, gbc = _hgrn_gates(abc, lb_b)
        s_f = gla_final_state(kfc, vc, gfc)
        s_b = gla_final_state(_flip(kbc), _flip(vc), _flip(gbc))
        yc = None
    q, v, kf, gf, kb, gb, og = _hgrn_project(h, w_in, lb_f, lb_b)
    o_f, _ = gla_chunk_scan(q, kf, v, gf, s_f)
    o_b, _ = gla_chunk_scan(_flip(q), _flip(kb), _flip(v), _flip(gb), s_b)
    y = _hgrn_readout(o_f + _flip(o_b), og, norm_g, w_out)
    return y, yc


def setup_inputs(seed: int = 0) -> dict:
    key = jax.random.key(seed)
    ks = jax.random.split(key, 16)
    D = D_MODEL

    def nrm(k, shape, s):
        return jax.random.normal(k, shape, jnp.float32) * s

    col_scale = jnp.concatenate([jnp.ones((D,)), BETA * jnp.ones((D,)), jnp.ones((3 * D,))]).astype(jnp.float32)
    return {
        "x": nrm(ks[0], (BATCH, SEQ, D), 1.0),
        "c": nrm(ks[1], (BATCH, D), 1.0),
        "ctx": nrm(ks[2], (BATCH, CTX_LEN, D), 1.0),
        "c_ctx": nrm(ks[3], (D,), 1.0),
        "mod_w": nrm(ks[4], (DEPTH, D, N_SUB * 3 * D), 0.5 * D ** -0.5),
        "mod_b": nrm(ks[5], (DEPTH, N_SUB * 3 * D), 0.02),
        "ln_g": 1.0 + nrm(ks[6], (DEPTH, N_SUB, D), 0.02),
        "ln_b": nrm(ks[7], (DEPTH, N_SUB, D), 0.02),
        "ffn_w_in": nrm(ks[8], (DEPTH, 2, D, 2 * D_FF), BETA * D ** -0.5),
        "ffn_w_out": nrm(ks[9], (DEPTH, 2, D_FF, D), BETA * D_FF ** -0.5),
        "pool_w": nrm(ks[10], (N_POOL_LAYERS, N_POOL_GROUPS, POOL_G, POOL_G), BETA * POOL_G ** -0.5),
        "pool_scale": 1.0 + nrm(ks[11], (N_POOL_LAYERS, D), 0.02),
        "hgrn_w_in": nrm(ks[12], (N_HGRN_LAYERS, D, 5 * D), D ** -0.5) * col_scale,
        "hgrn_lb": 1.0 + nrm(ks[13], (2, DEPTH, D), 0.1),
        "hgrn_norm_g": 1.0 + nrm(ks[14], (N_HGRN_LAYERS, HGRN_V), 0.02),
        "hgrn_w_out": nrm(ks[15], (N_HGRN_LAYERS, D, D), BETA * D ** -0.5),
    }


def reference(x, c, ctx, c_ctx, mod_w, mod_b, ln_g, ln_b, ffn_w_in, ffn_w_out, pool_w, pool_scale,
              hgrn_w_in, hgrn_lb, hgrn_norm_g, hgrn_w_out):
    b = x.shape[0]
    p = jax.nn.softmax(hgrn_lb.astype(jnp.float32), axis=1)
    lower_bounds = jnp.cumsum(p, axis=1) - p[:, :1]
    sc = jax.nn.silu(c)
    sctx = jax.nn.silu(c_ctx)
    for i in range(DEPTH):
        last = i == DEPTH - 1
        kind = i % N_MIXERS
        j = i // N_MIXERS
        ctx_used = (not last) or kind == 1
        mx = (sc @ mod_w[i] + mod_b[i]).reshape(b, N_SUB, 3, D_MODEL)
        mc = (sctx @ mod_w[i] + mod_b[i]).reshape(N_SUB, 3, D_MODEL)
        x = residual(x, 0.5 * swiglu(mod_in(x, mx, 0), ffn_w_in[i, 0], ffn_w_out[i, 0]), mx, 0, ln_g[i, 0], ln_b[i, 0])
        if ctx_used:
            ctx = residual(ctx, 0.5 * swiglu(mod_in(ctx, mc, 0), ffn_w_in[i, 0], ffn_w_out[i, 0]), mc, 0, ln_g[i, 0], ln_b[i, 0])
        if kind == 0:
            y = pool_mixer(mod_in(x, mx, 1), pool_w[j], pool_scale[j], True)
            yc = None if last else pool_mixer(mod_in(ctx, mc, 1), pool_w[j], pool_scale[j], False)
        else:
            y, yc = hgrn_mixer(mod_in(x, mx, 1), mod_in(ctx, mc, 1), hgrn_w_in[j], lower_bounds[0, i],
                               lower_bounds[1, i], hgrn_norm_g[j], hgrn_w_out[j], not last)
        x = residual(x, y, mx, 1, ln_g[i, 1], ln_b[i, 1])
        if not last:
            ctx = residual(ctx, yc, mc, 1, ln_g[i, 1], ln_b[i, 1])
        x = residual(x, 0.5 * swiglu(mod_in(x, mx, 2), ffn_w_in[i, 1], ffn_w_out[i, 1]), mx, 2, ln_g[i, 2], ln_b[i, 2])
        if not last:
            ctx = residual(ctx, 0.5 * swiglu(mod_in(ctx, mc, 2), ffn_w_in[i, 1], ffn_w_out[i, 1]), mc, 2, ln_g[i, 2], ln_b[i, 2])
    return x
```

```python
import functools

import numpy as np
import jax
import jax.numpy as jnp
from jax import lax
from jax.experimental import pallas as pl
from jax.experimental.pallas import tpu as pltpu

GRID_W = 64
POOL_WINDOWS = (2, 4, 8, 16)
HEAD_DIM = 128
SCAN_CHUNK = 64
LN_EPS = 1e-5
RMS_EPS = 1e-6
LANES = 128
VMEM_LIMIT_BYTES = 56 * 1024 * 1024

F32 = jnp.float32
BF16 = jnp.bfloat16


def _params(*sem):
    return pltpu.CompilerParams(dimension_semantics=sem, vmem_limit_bytes=VMEM_LIMIT_BYTES)


def _largest_tile(n, cap):
    best = LANES
    for t in range(LANES, min(n, cap) + 1, LANES):
        if n % t == 0:
            best = t
    assert n % best == 0
    return best


def _mod_in(x, mod_ref):
    return x * (1.0 + mod_ref[1:2, :]) + mod_ref[0:1, :]


def _ln_residual(x, y, mod_ref, ln_ref, alpha):
    z = alpha * x + mod_ref[2:3, :] * y
    mu = jnp.mean(z, axis=-1, keepdims=True)
    zc = z - mu
    var = jnp.mean(zc * zc, axis=-1, keepdims=True)
    return zc * lax.rsqrt(var + LN_EPS) * ln_ref[0:1, :] + ln_ref[1:2, :]


def _sigmoid(a):
    return 1.0 / (1.0 + jnp.exp(-a))


def _mod_kernel(cb_ref, w_ref, b_ref, o_ref, *, tn):
    for who in range(2):
        cv = cb_ref[who]
        s = cv * _sigmoid(cv)
        for l in range(tn // LANES):
            sl = slice(l * LANES, (l + 1) * LANES)
            acc = jnp.sum(w_ref[0, :, sl] * s, axis=0, keepdims=True)
            o_ref[0, who:who + 1, sl] = acc + b_ref[0, :, sl]


def _modulation(cb, mod_w, mod_b):
    depth, d, n = mod_w.shape
    tn = _largest_tile(n, 1024)
    return pl.pallas_call(
        functools.partial(_mod_kernel, tn=tn),
        grid=(depth, n // tn),
        in_specs=[
            pl.BlockSpec((2, d, LANES), lambda i, j: (0, 0, 0)),
            pl.BlockSpec((1, d, tn), lambda i, j: (i, 0, j)),
            pl.BlockSpec((1, 1, tn), lambda i, j: (i, 0, j)),
        ],
        out_specs=pl.BlockSpec((1, 2, tn), lambda i, j: (i, 0, j)),
        out_shape=jax.ShapeDtypeStruct((depth, 2, n), F32),
        compiler_params=_params("arbitrary", "arbitrary"),
        name="modulation",
    )(cb, mod_w, mod_b.reshape(depth, 1, n))


def _ffn_kernel(x_ref, mod_ref, ln_ref, wa_ref, wu_ref, wo_ref, o_ref, h_ref, *, alpha, nf):
    f = pl.program_id(1)

    @pl.when(f == 0)
    def _():
        h_ref[...] = _mod_in(x_ref[...], mod_ref).astype(BF16)
        o_ref[...] = jnp.zeros_like(o_ref)

    h = h_ref[...]
    a = jnp.dot(h, wa_ref[...], preferred_element_type=F32)
    u = jnp.dot(h, wu_ref[...], preferred_element_type=F32)
    g = (a * _sigmoid(a) * u).astype(BF16)
    o_ref[...] += jnp.dot(g, wo_ref[...], preferred_element_type=F32)

    @pl.when(f == nf - 1)
    def _():
        o_ref[...] = _ln_residual(x_ref[...], 0.5 * o_ref[...], mod_ref, ln_ref, alpha)


def _ffn(x, mod, ln, w_in, w_out, alpha, tm):
    t, d = x.shape
    d_ff = w_out.shape[0]
    tm = min(tm, t)
    tf = _largest_tile(d_ff, 512)
    nf = d_ff // tf
    return pl.pallas_call(
        functools.partial(_ffn_kernel, alpha=alpha, nf=nf),
        grid=(t // tm, nf),
        in_specs=[
            pl.BlockSpec((tm, d), lambda i, f: (i, 0)),
            pl.BlockSpec((3, d), lambda i, f: (0, 0)),
            pl.BlockSpec((2, d), lambda i, f: (0, 0)),
            pl.BlockSpec((d, tf), lambda i, f: (0, f)),
            pl.BlockSpec((d, tf), lambda i, f: (0, nf + f)),
            pl.BlockSpec((tf, d), lambda i, f: (f, 0)),
        ],
        out_specs=pl.BlockSpec((tm, d), lambda i, f: (i, 0)),
        out_shape=jax.ShapeDtypeStruct((t, d), F32),
        scratch_shapes=[pltpu.VMEM((tm, d), BF16)],
        compiler_params=_params("parallel", "arbitrary"),
        name="swiglu_half_step",
    )(x, mod, ln, w_in, w_in, w_out)


def _pool_kernel(*refs, alpha, row_len, n_rows, rows_per_tile, halo_rows, pool_g):
    if halo_rows:
        x_ref, xp_ref, xn_ref, mod_ref, ln_ref, w_ref, sc_ref, o_ref = refs
    else:
        x_ref, mod_ref, ln_ref, w_ref, sc_ref, o_ref = refs
    i = pl.program_id(0)
    tm = rows_per_tile * row_len
    tok = lax.broadcasted_iota(jnp.int32, (tm, LANES), 0)
    col = tok % row_len
    row = i * rows_per_tile + tok // row_len
    reps = pool_g // LANES

    def widen(a):
        return jnp.concatenate([a] * reps, axis=1) if reps > 1 else a

    x = x_ref[...]
    ys = []
    for gi, win in enumerate(POOL_WINDOWS):
        lanes = slice(gi * pool_g, (gi + 1) * pool_g)
        before = win // 2
        h = x[:, lanes] * (1.0 + mod_ref[1:2, lanes]) + mod_ref[0:1, lanes]
        if halo_rows:
            nh = halo_rows * row_len
            keep_p = (i > 0).astype(F32)
            keep_n = (i < pl.num_programs(0) - 1).astype(F32)
            hp = (xp_ref[:, lanes] * (1.0 + mod_ref[1:2, lanes]) + mod_ref[0:1, lanes]) * keep_p
            hn = (xn_ref[:, lanes] * (1.0 + mod_ref[1:2, lanes]) + mod_ref[0:1, lanes]) * keep_n
            buf = jnp.concatenate([hp, h, hn], axis=0)
            vs = None
            for dr in range(-before, win - before):
                start = nh + dr * row_len
                term = buf[start:start + tm, :]
                vs = term if vs is None else vs + term
            lo = jnp.maximum(row - before, 0)
            hi = jnp.minimum(row - before + win, n_rows)
            cnt = (hi - lo).astype(F32)
        else:
            vs = h
            cnt = jnp.ones((tm, LANES), F32)
        hs = vs
        for dc in range(-before, win - before):
            if dc == 0:
                continue
            valid = jnp.logical_and(col + dc >= 0, col + dc < row_len)
            shifted = pltpu.roll(vs, (-dc) % tm, 0)
            hs = hs + jnp.where(widen(valid), shifted, 0.0)
        lo = jnp.maximum(col - before, 0)
        hi = jnp.minimum(col - before + win, row_len)
        cnt = cnt * (hi - lo).astype(F32)
        p = (hs / widen(cnt) - h).astype(BF16)
        ys.append(jnp.dot(p, w_ref[gi], preferred_element_type=F32))
    y = jnp.concatenate(ys, axis=1) * sc_ref[...]
    o_ref[...] = _ln_residual(x, y, mod_ref, ln_ref, alpha)


def _pool(x, mod, ln, w, scale, alpha, row_len, rows_per_tile):
    t, d = x.shape
    n_rows = t // row_len
    groups = len(POOL_WINDOWS)
    pool_g = d // groups
    halo_rows = 0 if n_rows == 1 else max(POOL_WINDOWS) // 2
    rows_per_tile = min(rows_per_tile, n_rows)
    tm = rows_per_tile * row_len
    nt = n_rows // rows_per_tile
    common = [
        pl.BlockSpec((3, d), lambda i: (0, 0)),
        pl.BlockSpec((2, d), lambda i: (0, 0)),
        pl.BlockSpec((groups, pool_g, pool_g), lambda i: (0, 0, 0)),
        pl.BlockSpec((1, d), lambda i: (0, 0)),
    ]
    if halo_rows:
        assert rows_per_tile % halo_rows == 0
        nh = halo_rows * row_len
        per = tm // nh
        last = t // nh - 1
        in_specs = [
            pl.BlockSpec((tm, d), lambda i: (i, 0)),
            pl.BlockSpec((nh, d), lambda i: (jnp.maximum(i * per - 1, 0), 0)),
            pl.BlockSpec((nh, d), lambda i: (jnp.minimum((i + 1) * per, last), 0)),
        ] + common
        args = (x, x, x, mod, ln, w, scale)
    else:
        in_specs = [pl.BlockSpec((tm, d), lambda i: (i, 0))] + common
        args = (x, mod, ln, w, scale)
    return pl.pallas_call(
        functools.partial(_pool_kernel, alpha=alpha, row_len=row_len, n_rows=n_rows,
                          rows_per_tile=rows_per_tile, halo_rows=halo_rows, pool_g=pool_g),
        grid=(nt,),
        in_specs=in_specs,
        out_specs=pl.BlockSpec((tm, d), lambda i: (i, 0)),
        out_shape=jax.ShapeDtypeStruct((t, d), F32),
        compiler_params=_params("parallel"),
        name="pool_mixer",
    )(*args)


def _proj_kernel(*refs, mode, layer, q_scale):
    if mode == "gate":
        x_ref, mod_ref, w_ref, lb_ref, k_ref, gh_ref, gl_ref = refs
    else:
        x_ref, mod_ref, w_ref, o_ref = refs
    h = _mod_in(x_ref[...], mod_ref).astype(BF16)
    z = jnp.dot(h, w_ref[...], preferred_element_type=F32)
    if mode == "silu":
        o_ref[...] = (z * _sigmoid(z) * q_scale).astype(BF16)
    elif mode == "linear":
        o_ref[...] = z.astype(BF16)
    else:
        raw = lb_ref[...]
        e = jnp.exp(raw - jnp.max(raw, axis=0, keepdims=True))
        p = e / jnp.sum(e, axis=0, keepdims=True)
        lb = jnp.sum(p[1:layer + 1, :], axis=0, keepdims=True) if layer > 0 else jnp.zeros_like(p[0:1, :])
        ez = jnp.exp(-jnp.abs(z))
        log_sig = jnp.minimum(z, 0.0) - jnp.log(1.0 + ez)
        x1 = jnp.log(lb)
        x2 = jnp.log(1.0 - lb) + log_sig
        g = jnp.maximum(x1, x2) + jnp.log(1.0 + jnp.exp(-jnp.abs(x1 - x2)))
        inv = 1.0 / (1.0 + ez)
        k = (1.0 - lb) * jnp.where(z >= 0.0, ez * inv, inv)
        gh = g.astype(BF16)
        k_ref[...] = k.astype(BF16)
        gh_ref[...] = gh
        gl_ref[...] = (g - gh.astype(F32)).astype(BF16)


def _proj(x, mod, w, mode, tm, lb_raw=None, layer=0, q_scale=1.0):
    t, d = x.shape
    n = w.shape[1]
    tm = min(tm, t)
    in_specs = [
        pl.BlockSpec((tm, d), lambda i: (i, 0)),
        pl.BlockSpec((3, d), lambda i: (0, 0)),
        pl.BlockSpec((d, n), lambda i: (0, 0)),
    ]
    out_spec = pl.BlockSpec((tm, n), lambda i: (i, 0))
    out_sds = jax.ShapeDtypeStruct((t, n), BF16)
    args = [x, mod, w]
    if mode == "gate":
        in_specs.append(pl.BlockSpec(lb_raw.shape, lambda i: (0, 0)))
        args.append(lb_raw)
        out_specs, out_shape = [out_spec] * 3, [out_sds] * 3
    else:
        out_specs, out_shape = out_spec, out_sds
    return pl.pallas_call(
        functools.partial(_proj_kernel, mode=mode, layer=layer, q_scale=q_scale),
        grid=(t // tm,),
        in_specs=in_specs,
        out_specs=out_specs,
        out_shape=out_shape,
        compiler_params=_params("parallel"),
        name="hgrn_proj_" + mode,
    )(*args)


def _scan_tables(reverse):
    c = SCAN_CHUNK
    pos = np.arange(c)[::-1] if reverse else np.arange(c)
    pt, pr = pos[:, None], pos[None, :]
    blocks = [pr <= pt, pr > pt]
    masks = [pt == pr]
    b = c
    while b >= 2:
        mid = (pt // b) * b + b // 2
        late = pt >= mid
        blocks.append(np.where(late, (pr >= mid) & (pr <= pt), (pr > pt) & (pr < mid)))
        mid_s = (pr // b) * b + b // 2
        masks.append((pt // b == pr // b) & late & (pr < mid_s))
        b //= 2
    w = np.concatenate(blocks, axis=0).astype(np.float32)
    m = np.stack(masks, axis=0).astype(np.float32)
    return jnp.asarray(w, BF16), jnp.asarray(m, F32)


def _dot_nt(a, b):
    return lax.dot_general(a, b, (((1,), (1,)), ((), ())), preferred_element_type=F32)


def _dot_tn(a, b):
    return lax.dot_general(a, b, (((0,), (0,)), ((), ())), preferred_element_type=F32)


def _scan_kernel(q_ref, k_ref, v_ref, gh_ref, gl_ref, s0_ref, w_ref, m_ref, o_ref, sf_ref, s_ref,
                 *, reverse, heads, n_chunks):
    j = pl.program_id(1)
    c = SCAN_CHUNK
    n_levels = m_ref.shape[0] - 1
    last_row = 0 if reverse else c - 1

    @pl.when(j == 0)
    def _():
        s_ref[...] = s0_ref[...]

    def chunk(ci, carry):
        cc = (n_chunks - 1 - ci) if reverse else ci
        rows = pl.ds(pl.multiple_of(cc * c, c), c)
        w = w_ref[...]
        e_all = jnp.exp(jnp.dot(w, gh_ref[rows, :], preferred_element_type=F32)
                        + jnp.dot(w, gl_ref[rows, :], preferred_element_type=F32))
        for hh in range(heads):
            lanes = slice(hh * HEAD_DIM, (hh + 1) * HEAD_DIM)
            q = q_ref[rows, lanes].astype(F32)
            k = k_ref[rows, lanes].astype(F32)
            v = v_ref[rows, lanes]
            e = e_all[:, lanes]
            st = s_ref[hh]
            o = _dot_nt((q * e[0:c]).astype(BF16), st.astype(BF16))
            a = _dot_nt(q.astype(BF16), k.astype(BF16)) * m_ref[0]
            for lv in range(n_levels):
                el = e[(2 + lv) * c:(3 + lv) * c]
                a = a + _dot_nt((q * el).astype(BF16), (k * el).astype(BF16)) * m_ref[1 + lv]
            o = o + jnp.dot(a.astype(BF16), v, preferred_element_type=F32)
            o_ref[rows, lanes] = o.astype(o_ref.dtype)
            u = _dot_tn(v, (k * e[c:2 * c]).astype(BF16))
            s_ref[hh] = st * e[last_row:last_row + 1] + u
        return carry

    lax.fori_loop(0, n_chunks, chunk, 0)

    @pl.when(j == pl.num_programs(1) - 1)
    def _():
        sf_ref[...] = s_ref[...]


def _scan(q, k, v, gh, gl, s0, reverse, tb, heads_per_block):
    t, d = q.shape
    n_heads = d // HEAD_DIM
    hb = min(heads_per_block, n_heads)
    tb = min(tb, t)
    nb = t // tb
    width = hb * HEAD_DIM
    w, m = _scan_tables(reverse)
    tok = (lambda h, j: (nb - 1 - j, h)) if reverse else (lambda h, j: (j, h))
    seq_spec = pl.BlockSpec((tb, width), tok)
    state_spec = pl.BlockSpec((hb, HEAD_DIM, HEAD_DIM), lambda h, j: (h, 0, 0))
    return pl.pallas_call(
        functools.partial(_scan_kernel, reverse=reverse, heads=hb, n_chunks=tb // SCAN_CHUNK),
        grid=(n_heads // hb, nb),
        in_specs=[seq_spec] * 5 + [
            state_spec,
            pl.BlockSpec(w.shape, lambda h, j: (0, 0)),
            pl.BlockSpec(m.shape, lambda h, j: (0, 0, 0)),
        ],
        out_specs=[seq_spec, state_spec],
        out_shape=[jax.ShapeDtypeStruct((t, d), BF16),
                   jax.ShapeDtypeStruct((n_heads, HEAD_DIM, HEAD_DIM), F32)],
        scratch_shapes=[pltpu.VMEM((hb, HEAD_DIM, HEAD_DIM), F32)],
        compiler_params=_params("parallel", "arbitrary"),
        name="hgrn_scan_bwd" if reverse else "hgrn_scan_fwd",
    )(q, k, v, gh, gl, s0, w, m)


def _readout_kernel(x_ref, of_ref, ob_ref, og_ref, ng_ref, w_ref, mod_ref, ln_ref, o_ref, *, alpha, n_heads):
    parts = []
    for hh in range(n_heads):
        lanes = slice(hh * HEAD_DIM, (hh + 1) * HEAD_DIM)
        o = of_ref[:, lanes].astype(F32) + ob_ref[:, lanes].astype(F32)
        ms = jnp.mean(o * o, axis=-1, keepdims=True)
        r = o * lax.rsqrt(ms + RMS_EPS) * ng_ref[...]
        parts.append((r * og_ref[:, lanes].astype(F32)).astype(BF16))
    r = jnp.concatenate(parts, axis=1)
    y = jnp.dot(r, w_ref[...], preferred_element_type=F32)
    o_ref[...] = _ln_residual(x_ref[...], y, mod_ref, ln_ref, alpha)


def _readout(x, o_f, o_b, og, norm_g, w_out, mod, ln, alpha, tm):
    t, d = x.shape
    tm = min(tm, t)
    tile = pl.BlockSpec((tm, d), lambda i: (i, 0))
    return pl.pallas_call(
        functools.partial(_readout_kernel, alpha=alpha, n_heads=d // HEAD_DIM),
        grid=(t // tm,),
        in_specs=[tile, tile, tile, tile,
                  pl.BlockSpec((1, HEAD_DIM), lambda i: (0, 0)),
                  pl.BlockSpec((d, d), lambda i: (0, 0)),
                  pl.BlockSpec((3, d), lambda i: (0, 0)),
                  pl.BlockSpec((2, d), lambda i: (0, 0))],
        out_specs=tile,
        out_shape=jax.ShapeDtypeStruct((t, d), F32),
        compiler_params=_params("parallel"),
        name="hgrn_readout",
    )(x, o_f, o_b, og, norm_g, w_out, mod, ln)


def kernel(x, c, ctx, c_ctx, mod_w, mod_b, ln_g, ln_b, ffn_w_in, ffn_w_out, pool_w, pool_scale,
           hgrn_w_in, hgrn_lb, hgrn_norm_g, hgrn_w_out):
    batch, seq, d = x.shape
    assert batch == 1 and c.shape[0] == 1
    depth = mod_w.shape[0]
    alpha = float((2 * depth) ** 0.25)
    n_sub = mod_w.shape[2] // (3 * d)
    n_heads = d // HEAD_DIM
    tm = 512

    xs = x.reshape(seq, d)
    cs = ctx.reshape(ctx.shape[1], d)
    cb = jnp.broadcast_to(jnp.stack([c.reshape(d), c_ctx.reshape(d)])[:, :, None], (2, d, LANES))
    mods = _modulation(cb, mod_w, mod_b).reshape(depth, 2, n_sub, 3, d)
    ln = jnp.stack([ln_g, ln_b], axis=2)
    w_in_b = ffn_w_in.astype(BF16)
    w_out_b = ffn_w_out.astype(BF16)

    for i in range(depth):
        last = i == depth - 1
        kind = i % 2
        jm = i // 2
        ctx_used = (not last) or kind == 1
        mx, mc = mods[i, 0], mods[i, 1]

        xs = _ffn(xs, mx[0], ln[i, 0], w_in_b[i, 0], w_out_b[i, 0], alpha, tm)
        if ctx_used:
            cs = _ffn(cs, mc[0], ln[i, 0], w_in_b[i, 0], w_out_b[i, 0], alpha, tm)

        if kind == 0:
            pw = pool_w[jm].astype(BF16)
            ps = pool_scale[jm].reshape(1, d)
            xs_new = _pool(xs, mx[1], ln[i, 1], pw, ps, alpha, GRID_W, 8)
            if not last:
                cs = _pool(cs, mc[1], ln[i, 1], pw, ps, alpha, cs.shape[0], 1)
            xs = xs_new
        else:
            wh = hgrn_w_in[jm].astype(BF16)
            wq, wi, wf, wb, wg = (wh[:, s * d:(s + 1) * d] for s in range(5))
            zero = jnp.zeros((n_heads, HEAD_DIM, HEAD_DIM), F32)
            vc = _proj(cs, mc[1], wi, "linear", tm)
            kfc, gfhc, gflc = _proj(cs, mc[1], wf, "gate", tm, hgrn_lb[0], i)
            kbc, gbhc, gblc = _proj(cs, mc[1], wb, "gate", tm, hgrn_lb[1], i)
            if not last:
                qc = _proj(cs, mc[1], wq, "silu", tm, q_scale=HEAD_DIM ** -0.5)
                ogc = _proj(cs, mc[1], wg, "silu", tm)
            else:
                qc = vc
            tbc = min(256, cs.shape[0])
            ocf, s_f = _scan(qc, kfc, vc, gfhc, gflc, zero, False, tbc, 4)
            ocb, s_b = _scan(qc, kbc, vc, gbhc, gblc, zero, True, tbc, 4)
            q = _proj(xs, mx[1], wq, "silu", tm, q_scale=HEAD_DIM ** -0.5)
            v = _proj(xs, mx[1], wi, "linear", tm)
            kf, gfh, gfl = _proj(xs, mx[1], wf, "gate", tm, hgrn_lb[0], i)
            kb, gbh, gbl = _proj(xs, mx[1], wb, "gate", tm, hgrn_lb[1], i)
            og = _proj(xs, mx[1], wg, "silu", tm)
            o_f, _ = _scan(q, kf, v, gfh, gfl, s_f, False, 256, 4)
            o_b, _ = _scan(q, kb, v, gbh, gbl, s_b, True, 256, 4)
            ng = hgrn_norm_g[jm].reshape(1, HEAD_DIM)
            wo = hgrn_w_out[jm].astype(BF16)
            xs_new = _readout(xs, o_f, o_b, og, ng, wo, mx[1], ln[i, 1], alpha, tm)
            if not last:
                cs = _readout(cs, ocf, ocb, ogc, ng, wo, mc[1], ln[i, 1], alpha, tm)
            xs = xs_new

        xs = _ffn(xs, mx[2], ln[i, 2], w_in_b[i, 1], w_out_b[i, 1], alpha, tm)
        if not last:
            cs = _ffn(cs, mc[2], ln[i, 2], w_in_b[i, 1], w_out_b[i, 1], alpha, tm)

    return xs.reshape(batch, seq, d)
```

```python
import functools

import numpy as np
import jax
import jax.numpy as jnp
from jax import lax
from jax.experimental import pallas as pl
from jax.experimental.pallas import tpu as pltpu

GRID_W = 64
POOL_WINDOWS = (2, 4, 8, 16)
HEAD_DIM = 128
SCAN_CHUNK = 64
LN_EPS = 1e-5
RMS_EPS = 1e-6
LANES = 128
VMEM_LIMIT_BYTES = 56 * 1024 * 1024

F32 = jnp.float32
BF16 = jnp.bfloat16


def _params(*sem):
    return pltpu.CompilerParams(dimension_semantics=sem, vmem_limit_bytes=VMEM_LIMIT_BYTES)


def _largest_tile(n, cap):
    best = LANES
    for t in range(LANES, min(n, cap) + 1, LANES):
        if n % t == 0:
            best = t
    assert n % best == 0
    return best


def _mod_in(x, mod_ref):
    return x * (1.0 + mod_ref[1:2, :]) + mod_ref[0:1, :]


def _ln_residual(x, y, mod_ref, ln_ref, alpha):
    z = alpha * x + mod_ref[2:3, :] * y
    mu = jnp.mean(z, axis=-1, keepdims=True)
    zc = z - mu
    var = jnp.mean(zc * zc, axis=-1, keepdims=True)
    return zc * lax.rsqrt(var + LN_EPS) * ln_ref[0:1, :] + ln_ref[1:2, :]


def _sigmoid(a):
    return 1.0 / (1.0 + jnp.exp(-a))


def _mod_kernel(cb_ref, w_ref, b_ref, o_ref, *, tn):
    for who in range(2):
        cv = cb_ref[who]
        s = cv * _sigmoid(cv)
        for l in range(tn // LANES):
            sl = slice(l * LANES, (l + 1) * LANES)
            acc = jnp.sum(w_ref[0, :, sl] * s, axis=0, keepdims=True)
            o_ref[0, who:who + 1, sl] = acc + b_ref[0, :, sl]


def _modulation(cb, mod_w, mod_b):
    depth, d, n = mod_w.shape
    tn = _largest_tile(n, 1024)
    return pl.pallas_call(
        functools.partial(_mod_kernel, tn=tn),
        grid=(depth, n // tn),
        in_specs=[
            pl.BlockSpec((2, d, LANES), lambda i, j: (0, 0, 0)),
            pl.BlockSpec((1, d, tn), lambda i, j: (i, 0, j)),
            pl.BlockSpec((1, 1, tn), lambda i, j: (i, 0, j)),
        ],
        out_specs=pl.BlockSpec((1, 2, tn), lambda i, j: (i, 0, j)),
        out_shape=jax.ShapeDtypeStruct((depth, 2, n), F32),
        compiler_params=_params("arbitrary", "arbitrary"),
        name="modulation",
    )(cb, mod_w, mod_b.reshape(depth, 1, n))


def _ffn_kernel(x_ref, mod_ref, ln_ref, wa_ref, wu_ref, wo_ref, o_ref, h_ref, *, alpha, nf):
    f = pl.program_id(1)

    @pl.when(f == 0)
    def _():
        h_ref[...] = _mod_in(x_ref[...], mod_ref).astype(BF16)
        o_ref[...] = jnp.zeros_like(o_ref)

    h = h_ref[...]
    a = jnp.dot(h, wa_ref[...], preferred_element_type=F32)
    u = jnp.dot(h, wu_ref[...], preferred_element_type=F32)
    g = (a * _sigmoid(a) * u).astype(BF16)
    o_ref[...] += jnp.dot(g, wo_ref[...], preferred_element_type=F32)

    @pl.when(f == nf - 1)
    def _():
        o_ref[...] = _ln_residual(x_ref[...], 0.5 * o_ref[...], mod_ref, ln_ref, alpha)


def _ffn(x, mod, ln, w_in, w_out, alpha, tm):
    t, d = x.shape
    d_ff = w_out.shape[0]
    tm = min(tm, t)
    tf = _largest_tile(d_ff, 512)
    nf = d_ff // tf
    return pl.pallas_call(
        functools.partial(_ffn_kernel, alpha=alpha, nf=nf),
        grid=(t // tm, nf),
        in_specs=[
            pl.BlockSpec((tm, d), lambda i, f: (i, 0)),
            pl.BlockSpec((3, d), lambda i, f: (0, 0)),
            pl.BlockSpec((2, d), lambda i, f: (0, 0)),
            pl.BlockSpec((d, tf), lambda i, f: (0, f)),
            pl.BlockSpec((d, tf), lambda i, f: (0, nf + f)),
            pl.BlockSpec((tf, d), lambda i, f: (f, 0)),
        ],
        out_specs=pl.BlockSpec((tm, d), lambda i, f: (i, 0)),
        out_shape=jax.ShapeDtypeStruct((t, d), F32),
        scratch_shapes=[pltpu.VMEM((tm, d), BF16)],
        compiler_params=_params("parallel", "arbitrary"),
        name="swiglu_half_step",
    )(x, mod, ln, w_in, w_in, w_out)


def _pool_kernel(*refs, alpha, row_len, n_rows, rows_per_tile, halo_rows, pool_g):
    if halo_rows:
        x_ref, xp_ref, xn_ref, mod_ref, ln_ref, w_ref, sc_ref, o_ref = refs
    else:
        x_ref, mod_ref, ln_ref, w_ref, sc_ref, o_ref = refs
    i = pl.program_id(0)
    tm = rows_per_tile * row_len
    tok = lax.broadcasted_iota(jnp.int32, (tm, LANES), 0)
    col = tok % row_len
    row = i * rows_per_tile + tok // row_len
    reps = pool_g // LANES

    def widen(a):
        return jnp.concatenate([a] * reps, axis=1) if reps > 1 else a

    x = x_ref[...]
    ys = []
    for gi, win in enumerate(POOL_WINDOWS):
        lanes = slice(gi * pool_g, (gi + 1) * pool_g)
        before = win // 2
        h = x[:, lanes] * (1.0 + mod_ref[1:2, lanes]) + mod_ref[0:1, lanes]
        if halo_rows:
            nh = halo_rows * row_len
            keep_p = (i > 0).astype(F32)
            keep_n = (i < pl.num_programs(0) - 1).astype(F32)
            hp = (xp_ref[:, lanes] * (1.0 + mod_ref[1:2, lanes]) + mod_ref[0:1, lanes]) * keep_p
            hn = (xn_ref[:, lanes] * (1.0 + mod_ref[1:2, lanes]) + mod_ref[0:1, lanes]) * keep_n
            buf = jnp.concatenate([hp, h, hn], axis=0)
            vs = None
            for dr in range(-before, win - before):
                start = nh + dr * row_len
                term = buf[start:start + tm, :]
                vs = term if vs is None else vs + term
            lo = jnp.maximum(row - before, 0)
            hi = jnp.minimum(row - before + win, n_rows)
            cnt = (hi - lo).astype(F32)
        else:
            vs = h
            cnt = jnp.ones((tm, LANES), F32)
        hs = vs
        for dc in range(-before, win - before):
            if dc == 0:
                continue
            valid = jnp.logical_and(col + dc >= 0, col + dc < row_len)
            shifted = pltpu.roll(vs, (-dc) % tm, 0)
            hs = hs + jnp.where(widen(valid), shifted, 0.0)
        lo = jnp.maximum(col - before, 0)
        hi = jnp.minimum(col - before + win, row_len)
        cnt = cnt * (hi - lo).astype(F32)
        p = (hs / widen(cnt) - h).astype(BF16)
        ys.append(jnp.dot(p, w_ref[gi], preferred_element_type=F32))
    y = jnp.concatenate(ys, axis=1) * sc_ref[...]
    o_ref[...] = _ln_residual(x, y, mod_ref, ln_ref, alpha)


def _pool(x, mod, ln, w, scale, alpha, row_len, rows_per_tile):
    t, d = x.shape
    n_rows = t // row_len
    groups = len(POOL_WINDOWS)
    pool_g = d // groups
    halo_rows = 0 if n_rows == 1 else max(POOL_WINDOWS) // 2
    rows_per_tile = min(rows_per_tile, n_rows)
    tm = rows_per_tile * row_len
    nt = n_rows // rows_per_tile
    common = [
        pl.BlockSpec((3, d), lambda i: (0, 0)),
        pl.BlockSpec((2, d), lambda i: (0, 0)),
        pl.BlockSpec((groups, pool_g, pool_g), lambda i: (0, 0, 0)),
        pl.BlockSpec((1, d), lambda i: (0, 0)),
    ]
    if halo_rows:
        assert rows_per_tile % halo_rows == 0
        nh = halo_rows * row_len
        per = tm // nh
        last = t // nh - 1
        in_specs = [
            pl.BlockSpec((tm, d), lambda i: (i, 0)),
            pl.BlockSpec((nh, d), lambda i: (jnp.maximum(i * per - 1, 0), 0)),
            pl.BlockSpec((nh, d), lambda i: (jnp.minimum((i + 1) * per, last), 0)),
        ] + common
        args = (x, x, x, mod, ln, w, scale)
    else:
        in_specs = [pl.BlockSpec((tm, d), lambda i: (i, 0))] + common
        args = (x, mod, ln, w, scale)
    return pl.pallas_call(
        functools.partial(_pool_kernel, alpha=alpha, row_len=row_len, n_rows=n_rows,
                          rows_per_tile=rows_per_tile, halo_rows=halo_rows, pool_g=pool_g),
        grid=(nt,),
        in_specs=in_specs,
        out_specs=pl.BlockSpec((tm, d), lambda i: (i, 0)),
        out_shape=jax.ShapeDtypeStruct((t, d), F32),
        compiler_params=_params("parallel"),
        name="pool_mixer",
    )(*args)


def _proj_kernel(*refs, mode, layer, q_scale):
    if mode == "gate":
        x_ref, mod_ref, w_ref, lb_ref, k_ref, g2_ref = refs
    else:
        x_ref, mod_ref, w_ref, o_ref = refs
    h = _mod_in(x_ref[...], mod_ref).astype(BF16)
    z = jnp.dot(h, w_ref[...], preferred_element_type=F32)
    if mode == "silu":
        o_ref[...] = (z * _sigmoid(z) * q_scale).astype(BF16)
    elif mode == "linear":
        o_ref[...] = z.astype(BF16)
    else:
        raw = lb_ref[...]
        e = jnp.exp(raw - jnp.max(raw, axis=0, keepdims=True))
        p = e / jnp.sum(e, axis=0, keepdims=True)
        lb = jnp.sum(p[1:layer + 1, :], axis=0, keepdims=True) if layer > 0 else jnp.zeros_like(p[0:1, :])
        ez = jnp.exp(-jnp.abs(z))
        log_sig = jnp.minimum(z, 0.0) - jnp.log(1.0 + ez)
        x1 = jnp.log(lb)
        x2 = jnp.log(1.0 - lb) + log_sig
        g = jnp.maximum(x1, x2) + jnp.log(1.0 + jnp.exp(-jnp.abs(x1 - x2)))
        inv = 1.0 / (1.0 + ez)
        k = (1.0 - lb) * jnp.where(z >= 0.0, ez * inv, inv)
        gh = g.astype(BF16)
        gl = (g - gh.astype(F32)).astype(BF16)
        c = SCAN_CHUNK
        k_ref[...] = k.astype(BF16)
        g2_ref[:, 0:c, :] = gh.reshape(g2_ref.shape[0], c, g.shape[1])
        g2_ref[:, c:2 * c, :] = gl.reshape(g2_ref.shape[0], c, g.shape[1])


def _proj(x, mod, w, mode, tm, lb_raw=None, layer=0, q_scale=1.0):
    t, d = x.shape
    n = w.shape[1]
    tm = min(tm, t)
    in_specs = [
        pl.BlockSpec((tm, d), lambda i: (i, 0)),
        pl.BlockSpec((3, d), lambda i: (0, 0)),
        pl.BlockSpec((d, n), lambda i: (0, 0)),
    ]
    out_spec = pl.BlockSpec((tm, n), lambda i: (i, 0))
    out_sds = jax.ShapeDtypeStruct((t, n), BF16)
    args = [x, mod, w]
    if mode == "gate":
        in_specs.append(pl.BlockSpec(lb_raw.shape, lambda i: (0, 0)))
        args.append(lb_raw)
        c = SCAN_CHUNK
        out_specs = [out_spec, pl.BlockSpec((tm // c, 2 * c, n), lambda i: (i, 0, 0))]
        out_shape = [out_sds, jax.ShapeDtypeStruct((t // c, 2 * c, n), BF16)]
    else:
        out_specs, out_shape = out_spec, out_sds
    return pl.pallas_call(
        functools.partial(_proj_kernel, mode=mode, layer=layer, q_scale=q_scale),
        grid=(t // tm,),
        in_specs=in_specs,
        out_specs=out_specs,
        out_shape=out_shape,
        compiler_params=_params("parallel"),
        name="hgrn_proj_" + mode,
    )(*args)


SCAN_SMALL_BLOCK = 8
SAFE_LOG_DECAY = -80.0


def _scan_tables(reverse):
    c = SCAN_CHUNK
    pos = np.arange(c)[::-1] if reverse else np.arange(c)
    pt, pr = pos[:, None], pos[None, :]
    blocks = [pr <= pt]
    masks = [pt == pr]
    b = c
    while b >= 2:
        mid = (pt // b) * b + b // 2
        late = pt >= mid
        if b < SCAN_SMALL_BLOCK:
            blocks.append(np.where(late, (pr >= mid) & (pr <= pt), (pr > pt) & (pr < mid)))
        mid_s = (pr // b) * b + b // 2
        masks.append((pt // b == pr // b) & late & (pr < mid_s))
        b //= 2
    w = np.concatenate(blocks, axis=0).astype(np.float32)
    w = np.concatenate([w, w], axis=1)
    m = np.stack(masks, axis=0).astype(np.float32)
    return jnp.asarray(w, BF16), jnp.asarray(m, F32)


def _dot_nt(a, b):
    return lax.dot_general(a, b, (((1,), (1,)), ((), ())), preferred_element_type=F32)


def _dot_tn(a, b):
    return lax.dot_general(a, b, (((0,), (0,)), ((), ())), preferred_element_type=F32)


def _scan_kernel(q_ref, k_ref, v_ref, g2_ref, s0_ref, w_ref, m_ref, o_ref, sf_ref, s_ref,
                 *, reverse, heads, n_chunks):
    j = pl.program_id(1)
    c = SCAN_CHUNK
    width = heads * HEAD_DIM
    tot_row = 0 if reverse else c - 1

    @pl.when(j == 0)
    def _():
        s_ref[...] = s0_ref[...]

    n_small = w_ref.shape[0] // c - 1
    order = range(n_chunks - 1, -1, -1) if reverse else range(n_chunks)

    def finish(rows, lanes, hh, q_in, a, v, k_out, e_tot):
        st = s_ref[hh]
        o = _dot_nt(q_in, st.astype(BF16)) + jnp.dot(a.astype(BF16), v, preferred_element_type=F32)
        o_ref[rows, lanes] = o.astype(o_ref.dtype)
        s_ref[hh] = st * e_tot[:, lanes] + _dot_tn(v, k_out)

    tot_block = slice(0, 16) if reverse else slice(c - 16, c)
    w_tot = w_ref[tot_block, :]
    tots = [jnp.dot(w_tot, g2_ref[cc], preferred_element_type=F32)[tot_row % 16:tot_row % 16 + 1]
            for cc in order]
    safe = jnp.min(jnp.concatenate(tots, axis=0)) >= SAFE_LOG_DECAY

    @pl.when(safe)
    def _():
        pair_mask = jnp.sum(m_ref[...], axis=0) > 0.5
        w_cum = w_ref[0:c, :]
        units = [(cc, hh) for cc in order for hh in range(heads)]
        rows_of = lambda cc: slice(cc * c, (cc + 1) * c)
        lanes_of = lambda hh: slice(hh * HEAD_DIM, (hh + 1) * HEAD_DIM)
        cums = {cc: jnp.dot(w_cum, g2_ref[cc], preferred_element_type=F32) for cc in order}
        e_tot, q_in, k_inv, k_out = {}, {}, {}, {}
        for cc in order:
            cum = cums[cc]
            e_cum = jnp.exp(cum)
            e_tot[cc] = e_cum[tot_row:tot_row + 1]
            q_in[cc] = q_ref[rows_of(cc), :] * e_cum.astype(BF16)
            k = k_ref[rows_of(cc), :]
            k_inv[cc] = k * jnp.exp(-cum).astype(BF16)
            tot = jnp.broadcast_to(cum[tot_row:tot_row + 1], (c, width))
            k_out[cc] = k * jnp.exp(tot - cum).astype(BF16)
        pair = {(cc, hh): _dot_nt(q_in[cc][:, lanes_of(hh)], k_inv[cc][:, lanes_of(hh)]) for cc, hh in units}
        upd = {(cc, hh): _dot_tn(v_ref[rows_of(cc), lanes_of(hh)], k_out[cc][:, lanes_of(hh)])
               for cc, hh in units}
        intra = {(cc, hh): jnp.dot(jnp.where(pair_mask, pair[cc, hh], 0.0).astype(BF16),
                                   v_ref[rows_of(cc), lanes_of(hh)], preferred_element_type=F32)
                 for cc, hh in units}
        states = {}
        for hh in range(heads):
            st = s_ref[hh]
            for cc in order:
                states[cc, hh] = st.astype(BF16)
                st = st * e_tot[cc][:, lanes_of(hh)] + upd[cc, hh]
            s_ref[hh] = st
        for cc, hh in units:
            o = _dot_nt(q_in[cc][:, lanes_of(hh)], states[cc, hh]) + intra[cc, hh]
            o_ref[rows_of(cc), lanes_of(hh)] = o.astype(o_ref.dtype)

    @pl.when(jnp.logical_not(safe))
    def _():
        w = w_ref[...]
        masks = [m_ref[i] > 0.5 for i in range(m_ref.shape[0])]
        for cc in order:
            rows = slice(cc * c, (cc + 1) * c)
            sums = jnp.dot(w, g2_ref[cc], preferred_element_type=F32)
            cum = sums[0:c]
            exps = [cum, jnp.broadcast_to(cum[tot_row:tot_row + 1], (c, width)) - cum]
            b = c
            while b >= SCAN_SMALL_BLOCK:
                off = b // 2 if reverse else b // 2 - 1
                ref = jnp.concatenate(
                    [jnp.broadcast_to(cum[a + off:a + off + 1], (b, width)) for a in range(0, c, b)], axis=0)
                exps.append(-jnp.abs(cum - ref))
                b //= 2
            for i in range(n_small):
                exps.append(sums[(1 + i) * c:(2 + i) * c])
            e_f32 = jnp.exp(exps[0])
            e_tot = e_f32[tot_row:tot_row + 1]
            e_all = [e_f32.astype(BF16)] + [jnp.exp(x).astype(BF16) for x in exps[1:]]
            for hh in range(heads):
                lanes = slice(hh * HEAD_DIM, (hh + 1) * HEAD_DIM)
                q = q_ref[rows, lanes]
                k = k_ref[rows, lanes]
                a = jnp.where(masks[0], _dot_nt(q, k), 0.0)
                for lv in range(len(masks) - 1):
                    el = e_all[2 + lv][:, lanes]
                    a = jnp.where(masks[1 + lv], _dot_nt(q * el, k * el), a)
                finish(rows, lanes, hh, q * e_all[0][:, lanes], a, v_ref[rows, lanes],
                       k * e_all[1][:, lanes], e_tot)

    @pl.when(j == pl.num_programs(1) - 1)
    def _():
        sf_ref[...] = s_ref[...]


def _scan(q, k, v, g2, s0, reverse, tb, heads_per_block):
    t, d = q.shape
    c = SCAN_CHUNK
    n_heads = d // HEAD_DIM
    hb = min(heads_per_block, n_heads)
    tb = min(tb, t)
    nb = t // tb
    width = hb * HEAD_DIM
    w, m = _scan_tables(reverse)
    tok = (lambda h, j: (nb - 1 - j, h)) if reverse else (lambda h, j: (j, h))
    tok3 = (lambda h, j: (nb - 1 - j, 0, h)) if reverse else (lambda h, j: (j, 0, h))
    seq_spec = pl.BlockSpec((tb, width), tok)
    state_spec = pl.BlockSpec((hb, HEAD_DIM, HEAD_DIM), lambda h, j: (h, 0, 0))
    return pl.pallas_call(
        functools.partial(_scan_kernel, reverse=reverse, heads=hb, n_chunks=tb // c),
        grid=(n_heads // hb, nb),
        in_specs=[seq_spec] * 3 + [
            pl.BlockSpec((tb // c, 2 * c, width), tok3),
            state_spec,
            pl.BlockSpec(w.shape, lambda h, j: (0, 0)),
            pl.BlockSpec(m.shape, lambda h, j: (0, 0, 0)),
        ],
        out_specs=[seq_spec, state_spec],
        out_shape=[jax.ShapeDtypeStruct((t, d), BF16),
                   jax.ShapeDtypeStruct((n_heads, HEAD_DIM, HEAD_DIM), F32)],
        scratch_shapes=[pltpu.VMEM((hb, HEAD_DIM, HEAD_DIM), F32)],
        compiler_params=_params("parallel", "arbitrary"),
        name="hgrn_scan_bwd" if reverse else "hgrn_scan_fwd",
    )(q, k, v, g2, s0, w, m)


def _readout_kernel(x_ref, of_ref, ob_ref, og_ref, ng_ref, w_ref, mod_ref, ln_ref, o_ref, *, alpha, n_heads):
    parts = []
    for hh in range(n_heads):
        lanes = slice(hh * HEAD_DIM, (hh + 1) * HEAD_DIM)
        o = of_ref[:, lanes].astype(F32) + ob_ref[:, lanes].astype(F32)
        ms = jnp.mean(o * o, axis=-1, keepdims=True)
        r = o * lax.rsqrt(ms + RMS_EPS) * ng_ref[...]
        parts.append((r * og_ref[:, lanes].astype(F32)).astype(BF16))
    r = jnp.concatenate(parts, axis=1)
    y = jnp.dot(r, w_ref[...], preferred_element_type=F32)
    o_ref[...] = _ln_residual(x_ref[...], y, mod_ref, ln_ref, alpha)


def _readout(x, o_f, o_b, og, norm_g, w_out, mod, ln, alpha, tm):
    t, d = x.shape
    tm = min(tm, t)
    tile = pl.BlockSpec((tm, d), lambda i: (i, 0))
    return pl.pallas_call(
        functools.partial(_readout_kernel, alpha=alpha, n_heads=d // HEAD_DIM),
        grid=(t // tm,),
        in_specs=[tile, tile, tile, tile,
                  pl.BlockSpec((1, HEAD_DIM), lambda i: (0, 0)),
                  pl.BlockSpec((d, d), lambda i: (0, 0)),
                  pl.BlockSpec((3, d), lambda i: (0, 0)),
                  pl.BlockSpec((2, d), lambda i: (0, 0))],
        out_specs=tile,
        out_shape=jax.ShapeDtypeStruct((t, d), F32),
        compiler_params=_params("parallel"),
        name="hgrn_readout",
    )(x, o_f, o_b, og, norm_g, w_out, mod, ln)


def kernel(x, c, ctx, c_ctx, mod_w, mod_b, ln_g, ln_b, ffn_w_in, ffn_w_out, pool_w, pool_scale,
           hgrn_w_in, hgrn_lb, hgrn_norm_g, hgrn_w_out):
    batch, seq, d = x.shape
    assert batch == 1 and c.shape[0] == 1
    depth = mod_w.shape[0]
    alpha = float((2 * depth) ** 0.25)
    n_sub = mod_w.shape[2] // (3 * d)
    n_heads = d // HEAD_DIM
    tm = 512

    xs = x.reshape(seq, d)
    cs = ctx.reshape(ctx.shape[1], d)
    cb = jnp.broadcast_to(jnp.stack([c.reshape(d), c_ctx.reshape(d)])[:, :, None], (2, d, LANES))
    mods = _modulation(cb, mod_w, mod_b).reshape(depth, 2, n_sub, 3, d)
    ln = jnp.stack([ln_g, ln_b], axis=2)
    w_in_b = ffn_w_in.astype(BF16)
    w_out_b = ffn_w_out.astype(BF16)

    for i in range(depth):
        last = i == depth - 1
        kind = i % 2
        jm = i // 2
        ctx_used = (not last) or kind == 1
        mx, mc = mods[i, 0], mods[i, 1]

        xs = _ffn(xs, mx[0], ln[i, 0], w_in_b[i, 0], w_out_b[i, 0], alpha, tm)
        if ctx_used:
            cs = _ffn(cs, mc[0], ln[i, 0], w_in_b[i, 0], w_out_b[i, 0], alpha, tm)

        if kind == 0:
            pw = pool_w[jm].astype(BF16)
            ps = pool_scale[jm].reshape(1, d)
            xs_new = _pool(xs, mx[1], ln[i, 1], pw, ps, alpha, GRID_W, 8)
            if not last:
                cs = _pool(cs, mc[1], ln[i, 1], pw, ps, alpha, cs.shape[0], 1)
            xs = xs_new
        else:
            wh = hgrn_w_in[jm].astype(BF16)
            wq, wi, wf, wb, wg = (wh[:, s * d:(s + 1) * d] for s in range(5))
            zero = jnp.zeros((n_heads, HEAD_DIM, HEAD_DIM), F32)
            vc = _proj(cs, mc[1], wi, "linear", tm)
            kfc, gfc = _proj(cs, mc[1], wf, "gate", tm, hgrn_lb[0], i)
            kbc, gbc = _proj(cs, mc[1], wb, "gate", tm, hgrn_lb[1], i)
            if not last:
                qc = _proj(cs, mc[1], wq, "silu", tm, q_scale=HEAD_DIM ** -0.5)
                ogc = _proj(cs, mc[1], wg, "silu", tm)
            else:
                qc = vc
            tbc = min(256, cs.shape[0])
            ocf, s_f = _scan(qc, kfc, vc, gfc, zero, False, tbc, 4)
            ocb, s_b = _scan(qc, kbc, vc, gbc, zero, True, tbc, 4)
            q = _proj(xs, mx[1], wq, "silu", tm, q_scale=HEAD_DIM ** -0.5)
            v = _proj(xs, mx[1], wi, "linear", tm)
            kf, gf = _proj(xs, mx[1], wf, "gate", tm, hgrn_lb[0], i)
            kb, gb = _proj(xs, mx[1], wb, "gate", tm, hgrn_lb[1], i)
            og = _proj(xs, mx[1], wg, "silu", tm)
            o_f, _ = _scan(q, kf, v, gf, s_f, False, 256, 4)
            o_b, _ = _scan(q, kb, v, gb, s_b, True, 256, 4)
            ng = hgrn_norm_g[jm].reshape(1, HEAD_DIM)
            wo = hgrn_w_out[jm].astype(BF16)
            xs_new = _readout(xs, o_f, o_b, og, ng, wo, mx[1], ln[i, 1], alpha, tm)
            if not last:
                cs = _readout(cs, ocf, ocb, ogc, ng, wo, mc[1], ln[i, 1], alpha, tm)
            xs = xs_new

        xs = _ffn(xs, mx[2], ln[i, 2], w_in_b[i, 1], w_out_b[i, 1], alpha, tm)
        if not last:
            cs = _ffn(cs, mc[2], ln[i, 2], w_in_b[i, 1], w_out_b[i, 1], alpha, tm)

    return xs.reshape(batch, seq, d)
```

```python
import functools

import numpy as np
import jax
import jax.numpy as jnp
from jax import lax
from jax.experimental import pallas as pl
from jax.experimental.pallas import tpu as pltpu

GRID_W = 64
POOL_WINDOWS = (2, 4, 8, 16)
HEAD_DIM = 128
SCAN_CHUNK = 64
PROJ_ROW_CHUNK = 256
SCAN_BLOCK = 512
SCAN_HEADS = 4
SCAN_SMALL_BLOCK = 8
SAFE_LOG_DECAY = -80.0
LN_EPS = 1e-5
RMS_EPS = 1e-6
LANES = 128
VMEM_LIMIT_BYTES = 56 * 1024 * 1024

F32 = jnp.float32
BF16 = jnp.bfloat16


def _params(*sem):
    return pltpu.CompilerParams(dimension_semantics=sem, vmem_limit_bytes=VMEM_LIMIT_BYTES)


def _largest_tile(n, cap):
    best = LANES
    for t in range(LANES, min(n, cap) + 1, LANES):
        if n % t == 0:
            best = t
    assert n % best == 0
    return best


def _mod_in(x, mod_ref):
    return x * (1.0 + mod_ref[1:2, :]) + mod_ref[0:1, :]


def _ln_residual(x, y, mod_ref, ln_ref, alpha):
    z = alpha * x + mod_ref[2:3, :] * y
    mu = jnp.mean(z, axis=-1, keepdims=True)
    zc = z - mu
    var = jnp.mean(zc * zc, axis=-1, keepdims=True)
    return zc * lax.rsqrt(var + LN_EPS) * ln_ref[0:1, :] + ln_ref[1:2, :]


def _sigmoid(a):
    return 1.0 / (1.0 + jnp.exp(-a))


def _mod_kernel(cb_ref, w_ref, b_ref, o_ref, *, tn):
    for who in range(2):
        cv = cb_ref[who]
        s = cv * _sigmoid(cv)
        for l in range(tn // LANES):
            sl = slice(l * LANES, (l + 1) * LANES)
            acc = jnp.sum(w_ref[0, :, sl] * s, axis=0, keepdims=True)
            o_ref[0, who:who + 1, sl] = acc + b_ref[0, :, sl]


def _modulation(cb, mod_w, mod_b):
    depth, d, n = mod_w.shape
    tn = _largest_tile(n, 1024)
    return pl.pallas_call(
        functools.partial(_mod_kernel, tn=tn),
        grid=(depth, n // tn),
        in_specs=[
            pl.BlockSpec((2, d, LANES), lambda i, j: (0, 0, 0)),
            pl.BlockSpec((1, d, tn), lambda i, j: (i, 0, j)),
            pl.BlockSpec((1, 1, tn), lambda i, j: (i, 0, j)),
        ],
        out_specs=pl.BlockSpec((1, 2, tn), lambda i, j: (i, 0, j)),
        out_shape=jax.ShapeDtypeStruct((depth, 2, n), F32),
        compiler_params=_params("arbitrary", "arbitrary"),
        name="modulation",
    )(cb, mod_w, mod_b.reshape(depth, 1, n))


def _ffn_kernel(x_ref, mod_ref, ln_ref, wa_ref, wu_ref, wo_ref, o_ref, h_ref, *, alpha, nf):
    f = pl.program_id(1)

    @pl.when(f == 0)
    def _():
        h_ref[...] = _mod_in(x_ref[...], mod_ref).astype(BF16)
        o_ref[...] = jnp.zeros_like(o_ref)

    h = h_ref[...]
    a = jnp.dot(h, wa_ref[...], preferred_element_type=F32)
    u = jnp.dot(h, wu_ref[...], preferred_element_type=F32)
    g = (a * _sigmoid(a) * u).astype(BF16)
    o_ref[...] += jnp.dot(g, wo_ref[...], preferred_element_type=F32)

    @pl.when(f == nf - 1)
    def _():
        o_ref[...] = _ln_residual(x_ref[...], 0.5 * o_ref[...], mod_ref, ln_ref, alpha)


def _ffn(x, mod, ln, w_in, w_out, layer, half, alpha, tm):
    t, d = x.shape
    d_ff = w_out.shape[2]
    tm = min(tm, t)
    tf = _largest_tile(d_ff, 512)
    nf = d_ff // tf
    return pl.pallas_call(
        functools.partial(_ffn_kernel, alpha=alpha, nf=nf),
        grid=(t // tm, nf),
        in_specs=[
            pl.BlockSpec((tm, d), lambda i, f: (i, 0)),
            pl.BlockSpec((3, d), lambda i, f: (0, 0)),
            pl.BlockSpec((2, d), lambda i, f: (0, 0)),
            pl.BlockSpec((None, None, d, tf), lambda i, f: (layer, half, 0, f)),
            pl.BlockSpec((None, None, d, tf), lambda i, f: (layer, half, 0, nf + f)),
            pl.BlockSpec((None, None, tf, d), lambda i, f: (layer, half, f, 0)),
        ],
        out_specs=pl.BlockSpec((tm, d), lambda i, f: (i, 0)),
        out_shape=jax.ShapeDtypeStruct((t, d), F32),
        scratch_shapes=[pltpu.VMEM((tm, d), BF16)],
        compiler_params=_params("parallel", "arbitrary"),
        name="swiglu_half_step",
    )(x, mod, ln, w_in, w_in, w_out)


def _pool_kernel(*refs, alpha, row_len, n_rows, rows_per_tile, halo_rows, pool_g):
    if halo_rows:
        x_ref, xp_ref, xn_ref, mod_ref, ln_ref, w_ref, sc_ref, o_ref = refs
    else:
        x_ref, mod_ref, ln_ref, w_ref, sc_ref, o_ref = refs
    i = pl.program_id(0)
    tm = rows_per_tile * row_len
    tok = lax.broadcasted_iota(jnp.int32, (tm, LANES), 0)
    col = tok % row_len
    row = i * rows_per_tile + tok // row_len
    reps = pool_g // LANES

    def widen(a):
        return jnp.concatenate([a] * reps, axis=1) if reps > 1 else a

    x = x_ref[...]
    ys = []
    for gi, win in enumerate(POOL_WINDOWS):
        lanes = slice(gi * pool_g, (gi + 1) * pool_g)
        before = win // 2
        h = x[:, lanes] * (1.0 + mod_ref[1:2, lanes]) + mod_ref[0:1, lanes]
        if halo_rows:
            nh = halo_rows * row_len
            keep_p = (i > 0).astype(F32)
            keep_n = (i < pl.num_programs(0) - 1).astype(F32)
            hp = (xp_ref[:, lanes] * (1.0 + mod_ref[1:2, lanes]) + mod_ref[0:1, lanes]) * keep_p
            hn = (xn_ref[:, lanes] * (1.0 + mod_ref[1:2, lanes]) + mod_ref[0:1, lanes]) * keep_n
            buf = jnp.concatenate([hp, h, hn], axis=0)
            vs = None
            for dr in range(-before, win - before):
                start = nh + dr * row_len
                term = buf[start:start + tm, :]
                vs = term if vs is None else vs + term
            lo = jnp.maximum(row - before, 0)
            hi = jnp.minimum(row - before + win, n_rows)
            cnt = (hi - lo).astype(F32)
        else:
            vs = h
            cnt = jnp.ones((tm, LANES), F32)
        hs = vs
        for dc in range(-before, win - before):
            if dc == 0:
                continue
            valid = jnp.logical_and(col + dc >= 0, col + dc < row_len)
            shifted = pltpu.roll(vs, (-dc) % tm, 0)
            hs = hs + jnp.where(widen(valid), shifted, 0.0)
        lo = jnp.maximum(col - before, 0)
        hi = jnp.minimum(col - before + win, row_len)
        cnt = cnt * (hi - lo).astype(F32)
        p = (hs / widen(cnt) - h).astype(BF16)
        ys.append(jnp.dot(p, w_ref[gi], preferred_element_type=F32))
    y = jnp.concatenate(ys, axis=1) * sc_ref[...]
    o_ref[...] = _ln_residual(x, y, mod_ref, ln_ref, alpha)


def _pool(x, mod, ln, w, scale, alpha, row_len, rows_per_tile):
    t, d = x.shape
    n_rows = t // row_len
    groups = len(POOL_WINDOWS)
    pool_g = d // groups
    halo_rows = 0 if n_rows == 1 else max(POOL_WINDOWS) // 2
    rows_per_tile = min(rows_per_tile, n_rows)
    tm = rows_per_tile * row_len
    nt = n_rows // rows_per_tile
    common = [
        pl.BlockSpec((3, d), lambda i: (0, 0)),
        pl.BlockSpec((2, d), lambda i: (0, 0)),
        pl.BlockSpec((groups, pool_g, pool_g), lambda i: (0, 0, 0)),
        pl.BlockSpec((1, d), lambda i: (0, 0)),
    ]
    if halo_rows:
        assert rows_per_tile % halo_rows == 0
        nh = halo_rows * row_len
        per = tm // nh
        last = t // nh - 1
        in_specs = [
            pl.BlockSpec((tm, d), lambda i: (i, 0)),
            pl.BlockSpec((nh, d), lambda i: (jnp.maximum(i * per - 1, 0), 0)),
            pl.BlockSpec((nh, d), lambda i: (jnp.minimum((i + 1) * per, last), 0)),
        ] + common
        args = (x, x, x, mod, ln, w, scale)
    else:
        in_specs = [pl.BlockSpec((tm, d), lambda i: (i, 0))] + common
        args = (x, mod, ln, w, scale)
    return pl.pallas_call(
        functools.partial(_pool_kernel, alpha=alpha, row_len=row_len, n_rows=n_rows,
                          rows_per_tile=rows_per_tile, halo_rows=halo_rows, pool_g=pool_g),
        grid=(nt,),
        in_specs=in_specs,
        out_specs=pl.BlockSpec((tm, d), lambda i: (i, 0)),
        out_shape=jax.ShapeDtypeStruct((t, d), F32),
        compiler_params=_params("parallel"),
        name="pool_mixer",
    )(*args)


def _proj_kernel(*refs, mode, layer, q_scale):
    if mode == "gate":
        x_ref, mod_ref, w_ref, lb_ref, k_ref, g2_ref, safe_ref = refs
    else:
        x_ref, mod_ref, w_ref, o_ref = refs
    tm = x_ref.shape[0]
    rc = min(PROJ_ROW_CHUNK, tm)
    c = SCAN_CHUNK

    if mode == "gate":
        raw = lb_ref[...]
        e = jnp.exp(raw - jnp.max(raw, axis=0, keepdims=True))
        p = e / jnp.sum(e, axis=0, keepdims=True)
        lb = jnp.sum(p[1:layer + 1, :], axis=0, keepdims=True) if layer > 0 else jnp.zeros_like(p[0:1, :])
        log_lb = jnp.log(lb)
        log_1m_lb = jnp.log(1.0 - lb)

    def matmul(r):
        rows = slice(r * rc, (r + 1) * rc)
        h = _mod_in(x_ref[rows, :], mod_ref).astype(BF16)
        return jnp.dot(h, w_ref[...], preferred_element_type=F32)

    def epilogue(r, z):
        rows = slice(r * rc, (r + 1) * rc)
        if mode == "silu":
            o_ref[rows, :] = (z * _sigmoid(z) * q_scale).astype(BF16)
        elif mode == "linear":
            o_ref[rows, :] = z.astype(BF16)
        else:
            ez = jnp.exp(-jnp.abs(z))
            one_ez = 1.0 + ez
            x2 = log_1m_lb + (jnp.minimum(z, 0.0) - jnp.log(one_ez))
            g = jnp.maximum(log_lb, x2) + jnp.log(1.0 + jnp.exp(-jnp.abs(log_lb - x2)))
            inv = 1.0 / one_ez
            k = (1.0 - lb) * jnp.where(z >= 0.0, ez * inv, inv)
            gh = g.astype(BF16)
            gl = (g - gh.astype(F32)).astype(BF16)
            k_ref[rows, :] = k.astype(BF16)
            chunks = slice(r * (rc // c), (r + 1) * (rc // c))
            g2_ref[chunks, 0:c, :] = gh.reshape(rc // c, c, g.shape[1])
            g2_ref[chunks, c:2 * c, :] = gl.reshape(rc // c, c, g.shape[1])
            return jnp.min(jnp.sum(g.reshape(rc // c, c, g.shape[1]), axis=1), axis=0, keepdims=True)

    z = matmul(0)
    lowest = None
    for r in range(tm // rc):
        z_next = matmul(r + 1) if r + 1 < tm // rc else None
        low = epilogue(r, z)
        if mode == "gate":
            lowest = low if lowest is None else jnp.minimum(lowest, low)
        z = z_next

    if mode == "gate":
        width = SCAN_HEADS * HEAD_DIM
        lane = lax.broadcasted_iota(jnp.int32, safe_ref.shape[1:], 1)
        flags = jnp.zeros(safe_ref.shape[1:], jnp.int32)
        for b in range(lowest.shape[1] // width):
            ok = jnp.min(lowest[:, b * width:(b + 1) * width], axis=1, keepdims=True) >= SAFE_LOG_DECAY
            flags = jnp.where(jnp.logical_and(lane == b, ok), 1, flags)
        safe_ref[0] = flags


def _proj(x, mod, w, col, mode, tm, lb_raw=None, layer=0, q_scale=1.0):
    w, w_layer = w
    t, d = x.shape
    n = d
    tm = min(tm, t)
    in_specs = [
        pl.BlockSpec((tm, d), lambda i: (i, 0)),
        pl.BlockSpec((3, d), lambda i: (0, 0)),
        pl.BlockSpec((None, d, n), lambda i: (w_layer, 0, col)),
    ]
    out_spec = pl.BlockSpec((tm, n), lambda i: (i, 0))
    out_sds = jax.ShapeDtypeStruct((t, n), BF16)
    args = [x, mod, w]
    if mode == "gate":
        in_specs.append(pl.BlockSpec(lb_raw.shape, lambda i: (0, 0)))
        args.append(lb_raw)
        c = SCAN_CHUNK
        assert tm == min(SCAN_BLOCK, t)
        out_specs = [out_spec, pl.BlockSpec((tm // c, 2 * c, n), lambda i: (i, 0, 0)),
                     pl.BlockSpec((1, 8, LANES), lambda i: (i, 0, 0))]
        out_shape = [out_sds, jax.ShapeDtypeStruct((t // c, 2 * c, n), BF16),
                     jax.ShapeDtypeStruct((t // tm, 8, LANES), jnp.int32)]
    else:
        out_specs, out_shape = out_spec, out_sds
    return pl.pallas_call(
        functools.partial(_proj_kernel, mode=mode, layer=layer, q_scale=q_scale),
        grid=(t // tm,),
        in_specs=in_specs,
        out_specs=out_specs,
        out_shape=out_shape,
        compiler_params=_params("parallel"),
        name="hgrn_proj_" + mode,
    )(*args)


def _scan_tables(reverse):
    c = SCAN_CHUNK
    pos = np.arange(c)[::-1] if reverse else np.arange(c)
    pt, pr = pos[:, None], pos[None, :]
    blocks = [pr <= pt]
    masks = [pt == pr]
    b = c
    while b >= 2:
        mid = (pt // b) * b + b // 2
        late = pt >= mid
        if b < SCAN_SMALL_BLOCK:
            blocks.append(np.where(late, (pr >= mid) & (pr <= pt), (pr > pt) & (pr < mid)))
        mid_s = (pr // b) * b + b // 2
        masks.append((pt // b == pr // b) & late & (pr < mid_s))
        b //= 2
    w = np.concatenate(blocks, axis=0).astype(np.float32)
    w = np.concatenate([w, w], axis=1)
    m = np.stack(masks, axis=0).astype(np.float32)
    return jnp.asarray(w, BF16), jnp.asarray(m, F32)


def _dot_nt(a, b):
    return lax.dot_general(a, b, (((1,), (1,)), ((), ())), preferred_element_type=F32)


def _dot_tn(a, b):
    return lax.dot_general(a, b, (((0,), (0,)), ((), ())), preferred_element_type=F32)


def _scan_kernel(safe_ref, q_ref, k_ref, v_ref, g2_ref, s0_ref, w_ref, m_ref, o_ref, sf_ref, s_ref,
                 *, reverse, heads, n_chunks):
    j = pl.program_id(1)
    c = SCAN_CHUNK
    width = heads * HEAD_DIM
    tot_row = 0 if reverse else c - 1

    @pl.when(j == 0)
    def _():
        s_ref[...] = s0_ref[...]

    n_small = w_ref.shape[0] // c - 1
    order = range(n_chunks - 1, -1, -1) if reverse else range(n_chunks)

    def finish(rows, lanes, hh, q_in, a, v, k_out, e_tot):
        st = s_ref[hh]
        o = _dot_nt(q_in, st.astype(BF16)) + jnp.dot(a.astype(BF16), v, preferred_element_type=F32)
        o_ref[rows, lanes] = o.astype(o_ref.dtype)
        s_ref[hh] = st * e_tot[:, lanes] + _dot_tn(v, k_out)

    block = (pl.num_programs(1) - 1 - j) if reverse else j
    safe = safe_ref[block, pl.program_id(0)] == 1

    @pl.when(safe)
    def _():
        pair_mask = jnp.sum(m_ref[...], axis=0) > 0.5
        w_cum = w_ref[0:c, :]
        units = [(cc, hh) for cc in order for hh in range(heads)]
        rows_of = lambda cc: slice(cc * c, (cc + 1) * c)
        lanes_of = lambda hh: slice(hh * HEAD_DIM, (hh + 1) * HEAD_DIM)
        cums = {cc: jnp.dot(w_cum, g2_ref[cc], preferred_element_type=F32) for cc in order}
        e_tot, q_in, k_inv, k_out = {}, {}, {}, {}
        for cc in order:
            cum = cums[cc]
            e_cum = jnp.exp(cum)
            e_tot[cc] = e_cum[tot_row:tot_row + 1]
            q_in[cc] = q_ref[rows_of(cc), :] * e_cum.astype(BF16)
            k = k_ref[rows_of(cc), :]
            k_inv[cc] = k * jnp.exp(-cum).astype(BF16)
            tot = jnp.broadcast_to(cum[tot_row:tot_row + 1], (c, width))
            k_out[cc] = k * jnp.exp(tot - cum).astype(BF16)
        pair = {(cc, hh): _dot_nt(q_in[cc][:, lanes_of(hh)], k_inv[cc][:, lanes_of(hh)]) for cc, hh in units}
        upd = {(cc, hh): _dot_tn(v_ref[rows_of(cc), lanes_of(hh)], k_out[cc][:, lanes_of(hh)])
               for cc, hh in units}
        intra = {(cc, hh): jnp.dot(jnp.where(pair_mask, pair[cc, hh], 0.0).astype(BF16),
                                   v_ref[rows_of(cc), lanes_of(hh)], preferred_element_type=F32)
                 for cc, hh in units}
        states = {}
        for hh in range(heads):
            st = s_ref[hh]
            for cc in order:
                states[cc, hh] = st.astype(BF16)
                st = st * e_tot[cc][:, lanes_of(hh)] + upd[cc, hh]
            s_ref[hh] = st
        for cc, hh in units:
            o = _dot_nt(q_in[cc][:, lanes_of(hh)], states[cc, hh]) + intra[cc, hh]
            o_ref[rows_of(cc), lanes_of(hh)] = o.astype(o_ref.dtype)

    @pl.when(jnp.logical_not(safe))
    def _():
        w = w_ref[...]
        masks = [m_ref[i] > 0.5 for i in range(m_ref.shape[0])]

        @pl.loop(0, n_chunks)
        def _(ci):
            cc = (n_chunks - 1 - ci) if reverse else ci
            rows = pl.ds(pl.multiple_of(cc * c, c), c)
            sums = jnp.dot(w, g2_ref[cc], preferred_element_type=F32)
            cum = sums[0:c]
            exps = [cum, jnp.broadcast_to(cum[tot_row:tot_row + 1], (c, width)) - cum]
            b = c
            while b >= SCAN_SMALL_BLOCK:
                off = b // 2 if reverse else b // 2 - 1
                ref = jnp.concatenate(
                    [jnp.broadcast_to(cum[a + off:a + off + 1], (b, width)) for a in range(0, c, b)], axis=0)
                exps.append(-jnp.abs(cum - ref))
                b //= 2
            for i in range(n_small):
                exps.append(sums[(1 + i) * c:(2 + i) * c])
            e_f32 = jnp.exp(exps[0])
            e_tot = e_f32[tot_row:tot_row + 1]
            e_all = [e_f32.astype(BF16)] + [jnp.exp(x).astype(BF16) for x in exps[1:]]
            for hh in range(heads):
                lanes = slice(hh * HEAD_DIM, (hh + 1) * HEAD_DIM)
                q = q_ref[rows, lanes]
                k = k_ref[rows, lanes]
                a = jnp.where(masks[0], _dot_nt(q, k), 0.0)
                for lv in range(len(masks) - 1):
                    el = e_all[2 + lv][:, lanes]
                    a = jnp.where(masks[1 + lv], _dot_nt(q * el, k * el), a)
                finish(rows, lanes, hh, q * e_all[0][:, lanes], a, v_ref[rows, lanes],
                       k * e_all[1][:, lanes], e_tot)

    @pl.when(j == pl.num_programs(1) - 1)
    def _():
        sf_ref[...] = s_ref[...]


def _scan(q, k, v, g2, safe_tiles, s0, reverse):
    t, d = q.shape
    c = SCAN_CHUNK
    n_heads = d // HEAD_DIM
    hb = min(SCAN_HEADS, n_heads)
    tb = min(SCAN_BLOCK, t)
    nb = t // tb
    width = hb * HEAD_DIM
    w, m = _scan_tables(reverse)
    safe = safe_tiles[:, 0, :n_heads // hb]
    tok = (lambda h, j, s: (nb - 1 - j, h)) if reverse else (lambda h, j, s: (j, h))
    tok3 = (lambda h, j, s: (nb - 1 - j, 0, h)) if reverse else (lambda h, j, s: (j, 0, h))
    seq_spec = pl.BlockSpec((tb, width), tok)
    state_spec = pl.BlockSpec((hb, HEAD_DIM, HEAD_DIM), lambda h, j, s: (h, 0, 0))
    return pl.pallas_call(
        functools.partial(_scan_kernel, reverse=reverse, heads=hb, n_chunks=tb // c),
        grid_spec=pltpu.PrefetchScalarGridSpec(
            num_scalar_prefetch=1,
            grid=(n_heads // hb, nb),
            in_specs=[seq_spec] * 3 + [
                pl.BlockSpec((tb // c, 2 * c, width), tok3),
                state_spec,
                pl.BlockSpec(w.shape, lambda h, j, s: (0, 0)),
                pl.BlockSpec(m.shape, lambda h, j, s: (0, 0, 0)),
            ],
            out_specs=[seq_spec, state_spec],
            scratch_shapes=[pltpu.VMEM((hb, HEAD_DIM, HEAD_DIM), F32)],
        ),
        out_shape=[jax.ShapeDtypeStruct((t, d), BF16),
                   jax.ShapeDtypeStruct((n_heads, HEAD_DIM, HEAD_DIM), F32)],
        compiler_params=_params("parallel", "arbitrary"),
        name="hgrn_scan_bwd" if reverse else "hgrn_scan_fwd",
    )(safe, q, k, v, g2, s0, w, m)


def _readout_kernel(x_ref, of_ref, ob_ref, og_ref, ng_ref, w_ref, mod_ref, ln_ref, o_ref, *, alpha, n_heads):
    tm = x_ref.shape[0]
    rc = min(PROJ_ROW_CHUNK, tm)

    def matmul(r):
        rows = slice(r * rc, (r + 1) * rc)
        parts = []
        for hh in range(n_heads):
            lanes = slice(hh * HEAD_DIM, (hh + 1) * HEAD_DIM)
            o = of_ref[rows, lanes].astype(F32) + ob_ref[rows, lanes].astype(F32)
            ms = jnp.mean(o * o, axis=-1, keepdims=True)
            nrm = o * lax.rsqrt(ms + RMS_EPS) * ng_ref[...]
            parts.append((nrm * og_ref[rows, lanes].astype(F32)).astype(BF16))
        return jnp.dot(jnp.concatenate(parts, axis=1), w_ref[...], preferred_element_type=F32)

    y = matmul(0)
    for r in range(tm // rc):
        rows = slice(r * rc, (r + 1) * rc)
        y_next = matmul(r + 1) if r + 1 < tm // rc else None
        o_ref[rows, :] = _ln_residual(x_ref[rows, :], y, mod_ref, ln_ref, alpha)
        y = y_next


def _readout(x, o_f, o_b, og, norm_g, w_out, mod, ln, alpha, tm):
    t, d = x.shape
    tm = min(tm, t)
    tile = pl.BlockSpec((tm, d), lambda i: (i, 0))
    return pl.pallas_call(
        functools.partial(_readout_kernel, alpha=alpha, n_heads=d // HEAD_DIM),
        grid=(t // tm,),
        in_specs=[tile, tile, tile, tile,
                  pl.BlockSpec((1, HEAD_DIM), lambda i: (0, 0)),
                  pl.BlockSpec((d, d), lambda i: (0, 0)),
                  pl.BlockSpec((3, d), lambda i: (0, 0)),
                  pl.BlockSpec((2, d), lambda i: (0, 0))],
        out_specs=tile,
        out_shape=jax.ShapeDtypeStruct((t, d), F32),
        compiler_params=_params("parallel"),
        name="hgrn_readout",
    )(x, o_f, o_b, og, norm_g, w_out, mod, ln)


def kernel(x, c, ctx, c_ctx, mod_w, mod_b, ln_g, ln_b, ffn_w_in, ffn_w_out, pool_w, pool_scale,
           hgrn_w_in, hgrn_lb, hgrn_norm_g, hgrn_w_out):
    batch, seq, d = x.shape
    assert batch == 1 and c.shape[0] == 1
    depth = mod_w.shape[0]
    alpha = float((2 * depth) ** 0.25)
    n_sub = mod_w.shape[2] // (3 * d)
    n_heads = d // HEAD_DIM
    tm = 512

    xs = x.reshape(seq, d)
    cs = ctx.reshape(ctx.shape[1], d)
    cb = jnp.broadcast_to(jnp.stack([c.reshape(d), c_ctx.reshape(d)])[:, :, None], (2, d, LANES))
    mods = _modulation(cb, mod_w, mod_b).reshape(depth, 2, n_sub, 3, d)
    ln = jnp.stack([ln_g, ln_b], axis=2)
    w_in_b = ffn_w_in.astype(BF16)
    w_out_b = ffn_w_out.astype(BF16)
    w_hgrn_b = hgrn_w_in.astype(BF16)

    for i in range(depth):
        last = i == depth - 1
        kind = i % 2
        jm = i // 2
        ctx_used = (not last) or kind == 1
        mx, mc = mods[i, 0], mods[i, 1]

        xs = _ffn(xs, mx[0], ln[i, 0], w_in_b, w_out_b, i, 0, alpha, tm)
        if ctx_used:
            cs = _ffn(cs, mc[0], ln[i, 0], w_in_b, w_out_b, i, 0, alpha, tm)

        if kind == 0:
            pw = pool_w[jm].astype(BF16)
            ps = pool_scale[jm].reshape(1, d)
            xs_new = _pool(xs, mx[1], ln[i, 1], pw, ps, alpha, GRID_W, 8)
            if not last:
                cs = _pool(cs, mc[1], ln[i, 1], pw, ps, alpha, cs.shape[0], 1)
            xs = xs_new
        else:
            wh = (w_hgrn_b, jm)
            q_scale = HEAD_DIM ** -0.5
            zero = jnp.zeros((n_heads, HEAD_DIM, HEAD_DIM), F32)
            vc = _proj(cs, mc[1], wh, 1, "linear", tm)
            kfc, gfc, safe_fc = _proj(cs, mc[1], wh, 2, "gate", tm, hgrn_lb[0], i)
            kbc, gbc, safe_bc = _proj(cs, mc[1], wh, 3, "gate", tm, hgrn_lb[1], i)
            if not last:
                qc = _proj(cs, mc[1], wh, 0, "silu", tm, q_scale=q_scale)
                ogc = _proj(cs, mc[1], wh, 4, "silu", tm)
            else:
                qc = vc
            ocf, s_f = _scan(qc, kfc, vc, gfc, safe_fc, zero, False)
            ocb, s_b = _scan(qc, kbc, vc, gbc, safe_bc, zero, True)
            q = _proj(xs, mx[1], wh, 0, "silu", tm, q_scale=q_scale)
            v = _proj(xs, mx[1], wh, 1, "linear", tm)
            kf, gf, safe_f = _proj(xs, mx[1], wh, 2, "gate", tm, hgrn_lb[0], i)
            kb, gb, safe_b = _proj(xs, mx[1], wh, 3, "gate", tm, hgrn_lb[1], i)
            og = _proj(xs, mx[1], wh, 4, "silu", tm)
            o_f, _ = _scan(q, kf, v, gf, safe_f, s_f, False)
            o_b, _ = _scan(q, kb, v, gb, safe_b, s_b, True)
            ng = hgrn_norm_g[jm].reshape(1, HEAD_DIM)
            wo = hgrn_w_out[jm].astype(BF16)
            xs_new = _readout(xs, o_f, o_b, og, ng, wo, mx[1], ln[i, 1], alpha, tm)
            if not last:
                cs = _readout(cs, ocf, ocb, ogc, ng, wo, mc[1], ln[i, 1], alpha, tm)
            xs = xs_new

        xs = _ffn(xs, mx[2], ln[i, 2], w_in_b, w_out_b, i, 1, alpha, tm)
        if not last:
            cs = _ffn(cs, mc[2], ln[i, 2], w_in_b, w_out_b, i, 1, alpha, tm)

    return xs.reshape(batch, seq, d)
```

```python
import functools

import numpy as np
import jax
import jax.numpy as jnp
from jax import lax
from jax.experimental import pallas as pl
from jax.experimental.pallas import tpu as pltpu

GRID_W = 64
POOL_WINDOWS = (2, 4, 8, 16)
HEAD_DIM = 128
SCAN_CHUNK = 64
PROJ_ROW_CHUNK = 256
SCAN_BLOCK = 512
SCAN_HEADS = 4
SCAN_SMALL_BLOCK = 8
SAFE_LOG_DECAY = -80.0
NEG_LOG2_E = -1.4426950408889634
LN_EPS = 1e-5
RMS_EPS = 1e-6
LANES = 128
VMEM_LIMIT_BYTES = 60 * 1024 * 1024
FFN_ROWS = 1024
FFN_OUT_BLOCK = 512

F32 = jnp.float32
BF16 = jnp.bfloat16


def _params(*sem):
    return pltpu.CompilerParams(dimension_semantics=sem, vmem_limit_bytes=VMEM_LIMIT_BYTES)


def _largest_tile(n, cap):
    best = LANES
    for t in range(LANES, min(n, cap) + 1, LANES):
        if n % t == 0:
            best = t
    assert n % best == 0
    return best


def _mod_in(x, mod_ref):
    return x * (1.0 + mod_ref[1:2, :]) + mod_ref[0:1, :]


def _ln_residual(x, y, mod_ref, ln_ref, alpha):
    z = alpha * x + mod_ref[2:3, :] * y
    mu = jnp.mean(z, axis=-1, keepdims=True)
    zc = z - mu
    var = jnp.mean(zc * zc, axis=-1, keepdims=True)
    return zc * lax.rsqrt(var + LN_EPS) * ln_ref[0:1, :] + ln_ref[1:2, :]


def _sigmoid(a):
    return 1.0 / (1.0 + jnp.exp(-a))


def _mod_kernel(cb_ref, w_ref, b_ref, o_ref, *, tn):
    for who in range(2):
        cv = cb_ref[who]
        s = cv * _sigmoid(cv)
        for l in range(tn // LANES):
            sl = slice(l * LANES, (l + 1) * LANES)
            acc = jnp.sum(w_ref[0, :, sl] * s, axis=0, keepdims=True)
            o_ref[0, who:who + 1, sl] = acc + b_ref[0, :, sl]


def _modulation(cb, mod_w, mod_b):
    depth, d, n = mod_w.shape
    tn = _largest_tile(n, 1024)
    return pl.pallas_call(
        functools.partial(_mod_kernel, tn=tn),
        grid=(depth, n // tn),
        in_specs=[
            pl.BlockSpec((2, d, LANES), lambda i, j: (0, 0, 0)),
            pl.BlockSpec((1, d, tn), lambda i, j: (i, 0, j)),
            pl.BlockSpec((1, 1, tn), lambda i, j: (i, 0, j)),
        ],
        out_specs=pl.BlockSpec((1, 2, tn), lambda i, j: (i, 0, j)),
        out_shape=jax.ShapeDtypeStruct((depth, 2, n), F32),
        compiler_params=_params("arbitrary", "arbitrary"),
        name="modulation",
    )(cb, mod_w, mod_b.reshape(depth, 1, n))


def _ffn_kernel(x_ref, mod_ref, ln_ref, wa_ref, wu_ref, wo_ref, o_ref, h_ref, *, alpha, nf):
    f = pl.program_id(1)
    tm = x_ref.shape[0]

    def prologue(rows):
        h_ref[rows, :] = _mod_in(x_ref[rows, :], mod_ref).astype(BF16)

    def step(rows, first):
        h = h_ref[rows, :]
        a = jnp.dot(h, wa_ref[...], preferred_element_type=F32)
        u = jnp.dot(h, wu_ref[...], preferred_element_type=F32)
        g = (a * _sigmoid(a) * u).astype(BF16)
        for n in range(o_ref.shape[1] // FFN_OUT_BLOCK):
            cols = slice(n * FFN_OUT_BLOCK, (n + 1) * FFN_OUT_BLOCK)
            part = jnp.dot(g, wo_ref[:, cols], preferred_element_type=F32)
            if first:
                o_ref[rows, cols] = part
            else:
                o_ref[rows, cols] += part

    def epilogue(rows):
        o_ref[rows, :] = _ln_residual(x_ref[rows, :], 0.5 * o_ref[rows, :], mod_ref, ln_ref, alpha)

    halves = [slice(0, tm // 2), slice(tm // 2, tm)] if tm >= 512 else [slice(0, tm)]

    def edge_step(first, last):
        if first:
            prologue(halves[0])
        for i, rows in enumerate(halves):
            step(rows, first)
            if first and i + 1 < len(halves):
                prologue(halves[i + 1])
        if last:
            for rows in halves:
                epilogue(rows)

    if nf == 1:
        edge_step(True, True)
    else:
        pl.when(f == 0)(lambda: edge_step(True, False))
        pl.when(jnp.logical_and(f > 0, f < nf - 1))(lambda: step(slice(0, tm), False))
        pl.when(f == nf - 1)(lambda: edge_step(False, True))


def _ffn(x, mod, ln, w_in, w_out, layer, half, alpha, tm):
    t, d = x.shape
    d_ff = w_out.shape[2]
    tm = min(tm, t)
    tf = _largest_tile(d_ff, 512)
    nf = d_ff // tf
    return pl.pallas_call(
        functools.partial(_ffn_kernel, alpha=alpha, nf=nf),
        grid=(t // tm, nf),
        in_specs=[
            pl.BlockSpec((tm, d), lambda i, f: (i, 0)),
            pl.BlockSpec((3, d), lambda i, f: (0, 0)),
            pl.BlockSpec((2, d), lambda i, f: (0, 0)),
            pl.BlockSpec((None, None, d, tf), lambda i, f: (layer, half, 0, f)),
            pl.BlockSpec((None, None, d, tf), lambda i, f: (layer, half, 0, nf + f)),
            pl.BlockSpec((None, None, tf, d), lambda i, f: (layer, half, f, 0)),
        ],
        out_specs=pl.BlockSpec((tm, d), lambda i, f: (i, 0)),
        out_shape=jax.ShapeDtypeStruct((t, d), F32),
        scratch_shapes=[pltpu.VMEM((tm, d), BF16)],
        compiler_params=_params("parallel", "arbitrary"),
        name="swiglu_half_step",
    )(x, mod, ln, w_in, w_in, w_out)


def _pool_kernel(*refs, alpha, row_len, n_rows, rows_per_tile, halo_rows, pool_g):
    if halo_rows:
        x_ref, xp_ref, xn_ref, mod_ref, ln_ref, w_ref, sc_ref, o_ref = refs
    else:
        x_ref, mod_ref, ln_ref, w_ref, sc_ref, o_ref = refs
    i = pl.program_id(0)
    tm = rows_per_tile * row_len
    tok = lax.broadcasted_iota(jnp.int32, (tm, LANES), 0)
    col = tok % row_len
    row = i * rows_per_tile + tok // row_len
    reps = pool_g // LANES

    def widen(a):
        return jnp.concatenate([a] * reps, axis=1) if reps > 1 else a

    x = x_ref[...]
    ys = []
    for gi, win in enumerate(POOL_WINDOWS):
        lanes = slice(gi * pool_g, (gi + 1) * pool_g)
        before = win // 2
        h = x[:, lanes] * (1.0 + mod_ref[1:2, lanes]) + mod_ref[0:1, lanes]
        if halo_rows:
            nh = halo_rows * row_len
            keep_p = (i > 0).astype(F32)
            keep_n = (i < pl.num_programs(0) - 1).astype(F32)
            hp = (xp_ref[:, lanes] * (1.0 + mod_ref[1:2, lanes]) + mod_ref[0:1, lanes]) * keep_p
            hn = (xn_ref[:, lanes] * (1.0 + mod_ref[1:2, lanes]) + mod_ref[0:1, lanes]) * keep_n
            buf = jnp.concatenate([hp, h, hn], axis=0)
            vs = None
            for dr in range(-before, win - before):
                start = nh + dr * row_len
                term = buf[start:start + tm, :]
                vs = term if vs is None else vs + term
            lo = jnp.maximum(row - before, 0)
            hi = jnp.minimum(row - before + win, n_rows)
            cnt = (hi - lo).astype(F32)
        else:
            vs = h
            cnt = jnp.ones((tm, LANES), F32)
        hs = vs
        for dc in range(-before, win - before):
            if dc == 0:
                continue
            valid = jnp.logical_and(col + dc >= 0, col + dc < row_len)
            shifted = pltpu.roll(vs, (-dc) % tm, 0)
            hs = hs + jnp.where(widen(valid), shifted, 0.0)
        lo = jnp.maximum(col - before, 0)
        hi = jnp.minimum(col - before + win, row_len)
        cnt = cnt * (hi - lo).astype(F32)
        p = (hs / widen(cnt) - h).astype(BF16)
        ys.append(jnp.dot(p, w_ref[gi], preferred_element_type=F32))
    y = jnp.concatenate(ys, axis=1) * sc_ref[...]
    o_ref[...] = _ln_residual(x, y, mod_ref, ln_ref, alpha)


def _pool(x, mod, ln, w, scale, alpha, row_len, rows_per_tile):
    t, d = x.shape
    n_rows = t // row_len
    groups = len(POOL_WINDOWS)
    pool_g = d // groups
    halo_rows = 0 if n_rows == 1 else max(POOL_WINDOWS) // 2
    rows_per_tile = min(rows_per_tile, n_rows)
    tm = rows_per_tile * row_len
    nt = n_rows // rows_per_tile
    common = [
        pl.BlockSpec((3, d), lambda i: (0, 0)),
        pl.BlockSpec((2, d), lambda i: (0, 0)),
        pl.BlockSpec((groups, pool_g, pool_g), lambda i: (0, 0, 0)),
        pl.BlockSpec((1, d), lambda i: (0, 0)),
    ]
    if halo_rows:
        assert rows_per_tile % halo_rows == 0
        nh = halo_rows * row_len
        per = tm // nh
        last = t // nh - 1
        in_specs = [
            pl.BlockSpec((tm, d), lambda i: (i, 0)),
            pl.BlockSpec((nh, d), lambda i: (jnp.maximum(i * per - 1, 0), 0)),
            pl.BlockSpec((nh, d), lambda i: (jnp.minimum((i + 1) * per, last), 0)),
        ] + common
        args = (x, x, x, mod, ln, w, scale)
    else:
        in_specs = [pl.BlockSpec((tm, d), lambda i: (i, 0))] + common
        args = (x, mod, ln, w, scale)
    return pl.pallas_call(
        functools.partial(_pool_kernel, alpha=alpha, row_len=row_len, n_rows=n_rows,
                          rows_per_tile=rows_per_tile, halo_rows=halo_rows, pool_g=pool_g),
        grid=(nt,),
        in_specs=in_specs,
        out_specs=pl.BlockSpec((tm, d), lambda i: (i, 0)),
        out_shape=jax.ShapeDtypeStruct((t, d), F32),
        compiler_params=_params("parallel"),
        name="pool_mixer",
    )(*args)


def _proj_kernel(*refs, mode, layer, q_scale):
    if mode == "gate":
        x_ref, mod_ref, w_ref, lb_ref, k_ref, g2_ref, safe_ref = refs
    else:
        x_ref, mod_ref, w_ref, o_ref = refs
    tm = x_ref.shape[0]
    rc = min(PROJ_ROW_CHUNK, tm)
    c = SCAN_CHUNK

    if mode == "gate":
        raw = lb_ref[...]
        e = jnp.exp(raw - jnp.max(raw, axis=0, keepdims=True))
        p = e / jnp.sum(e, axis=0, keepdims=True)
        lb = jnp.sum(p[1:layer + 1, :], axis=0, keepdims=True) if layer > 0 else jnp.zeros_like(p[0:1, :])
        log_lb = jnp.log(lb)
        log_1m_lb = jnp.log(1.0 - lb)

    def matmul(r):
        rows = slice(r * rc, (r + 1) * rc)
        h = _mod_in(x_ref[rows, :], mod_ref).astype(BF16)
        return jnp.dot(h, w_ref[...], preferred_element_type=F32)

    def epilogue(r, z):
        rows = slice(r * rc, (r + 1) * rc)
        if mode == "silu":
            o_ref[rows, :] = (z * _sigmoid(z) * q_scale).astype(BF16)
        elif mode == "linear":
            o_ref[rows, :] = z.astype(BF16)
        else:
            ez = jnp.exp2(jnp.abs(z) * NEG_LOG2_E)
            one_ez = 1.0 + ez
            x2 = log_1m_lb + (jnp.minimum(z, 0.0) - jnp.log(one_ez))
            g = jnp.maximum(log_lb, x2) + jnp.log(1.0 + jnp.exp2(jnp.abs(log_lb - x2) * NEG_LOG2_E))
            inv = 1.0 / one_ez
            k = (1.0 - lb) * jnp.where(z >= 0.0, ez * inv, inv)
            gh = g.astype(BF16)
            gl = (g - gh.astype(F32)).astype(BF16)
            k_ref[rows, :] = k.astype(BF16)
            chunks = slice(r * (rc // c), (r + 1) * (rc // c))
            g2_ref[chunks, 0:c, :] = gh.reshape(rc // c, c, g.shape[1])
            g2_ref[chunks, c:2 * c, :] = gl.reshape(rc // c, c, g.shape[1])
            return jnp.min(jnp.sum(g.reshape(rc // c, c, g.shape[1]), axis=1), axis=0, keepdims=True)

    z = matmul(0)
    lowest = None
    for r in range(tm // rc):
        z_next = matmul(r + 1) if r + 1 < tm // rc else None
        low = epilogue(r, z)
        if mode == "gate":
            lowest = low if lowest is None else jnp.minimum(lowest, low)
        z = z_next

    if mode == "gate":
        width = SCAN_HEADS * HEAD_DIM
        lane = lax.broadcasted_iota(jnp.int32, safe_ref.shape[1:], 1)
        flags = jnp.zeros(safe_ref.shape[1:], jnp.int32)
        for b in range(lowest.shape[1] // width):
            ok = jnp.min(lowest[:, b * width:(b + 1) * width], axis=1, keepdims=True) >= SAFE_LOG_DECAY
            flags = jnp.where(jnp.logical_and(lane == b, ok), 1, flags)
        safe_ref[0] = flags


def _proj(x, mod, w, col, mode, tm, lb_raw=None, layer=0, q_scale=1.0):
    w, w_layer = w
    t, d = x.shape
    n = d
    tm = min(tm, t)
    in_specs = [
        pl.BlockSpec((tm, d), lambda i: (i, 0)),
        pl.BlockSpec((3, d), lambda i: (0, 0)),
        pl.BlockSpec((None, d, n), lambda i: (w_layer, 0, col)),
    ]
    out_spec = pl.BlockSpec((tm, n), lambda i: (i, 0))
    out_sds = jax.ShapeDtypeStruct((t, n), BF16)
    args = [x, mod, w]
    if mode == "gate":
        in_specs.append(pl.BlockSpec(lb_raw.shape, lambda i: (0, 0)))
        args.append(lb_raw)
        c = SCAN_CHUNK
        assert tm == min(SCAN_BLOCK, t)
        out_specs = [out_spec, pl.BlockSpec((tm // c, 2 * c, n), lambda i: (i, 0, 0)),
                     pl.BlockSpec((1, 8, LANES), lambda i: (i, 0, 0))]
        out_shape = [out_sds, jax.ShapeDtypeStruct((t // c, 2 * c, n), BF16),
                     jax.ShapeDtypeStruct((t // tm, 8, LANES), jnp.int32)]
    else:
        out_specs, out_shape = out_spec, out_sds
    return pl.pallas_call(
        functools.partial(_proj_kernel, mode=mode, layer=layer, q_scale=q_scale),
        grid=(t // tm,),
        in_specs=in_specs,
        out_specs=out_specs,
        out_shape=out_shape,
        compiler_params=_params("parallel"),
        name="hgrn_proj_" + mode,
    )(*args)


def _scan_tables(reverse):
    c = SCAN_CHUNK
    pos = np.arange(c)[::-1] if reverse else np.arange(c)
    pt, pr = pos[:, None], pos[None, :]
    blocks = [pr <= pt]
    masks = [pt == pr]
    b = c
    while b >= 2:
        mid = (pt // b) * b + b // 2
        late = pt >= mid
        if b < SCAN_SMALL_BLOCK:
            blocks.append(np.where(late, (pr >= mid) & (pr <= pt), (pr > pt) & (pr < mid)))
        mid_s = (pr // b) * b + b // 2
        masks.append((pt // b == pr // b) & late & (pr < mid_s))
        b //= 2
    w = np.concatenate(blocks, axis=0).astype(np.float32)
    w = np.concatenate([w, w], axis=1)
    m = np.stack(masks, axis=0).astype(np.float32)
    return jnp.asarray(w, BF16), jnp.asarray(m, F32)


def _dot_nt(a, b):
    return lax.dot_general(a, b, (((1,), (1,)), ((), ())), preferred_element_type=F32)


def _dot_tn(a, b):
    return lax.dot_general(a, b, (((0,), (0,)), ((), ())), preferred_element_type=F32)


def _scan_kernel(safe_ref, q_ref, k_ref, v_ref, g2_ref, s0_ref, w_ref, m_ref, o_ref, sf_ref, s_ref,
                 *, reverse, heads, n_chunks):
    j = pl.program_id(1)
    c = SCAN_CHUNK
    width = heads * HEAD_DIM
    tot_row = 0 if reverse else c - 1

    @pl.when(j == 0)
    def _():
        s_ref[...] = s0_ref[...]

    n_small = w_ref.shape[0] // c - 1
    order = range(n_chunks - 1, -1, -1) if reverse else range(n_chunks)

    def finish(rows, lanes, hh, q_in, a, v, k_out, e_tot):
        st = s_ref[hh]
        o = _dot_nt(q_in, st.astype(BF16)) + jnp.dot(a.astype(BF16), v, preferred_element_type=F32)
        o_ref[rows, lanes] = o.astype(o_ref.dtype)
        s_ref[hh] = st * e_tot[:, lanes] + _dot_tn(v, k_out)

    block = (pl.num_programs(1) - 1 - j) if reverse else j
    safe = safe_ref[block, pl.program_id(0)] == 1

    @pl.when(safe)
    def _():
        pair_mask = jnp.sum(m_ref[...], axis=0) > 0.5
        w_cum = w_ref[0:c, :]
        units = [(cc, hh) for cc in order for hh in range(heads)]
        rows_of = lambda cc: slice(cc * c, (cc + 1) * c)
        lanes_of = lambda hh: slice(hh * HEAD_DIM, (hh + 1) * HEAD_DIM)
        cums = {cc: jnp.dot(w_cum, g2_ref[cc], preferred_element_type=F32) for cc in order}
        e_tot, q_in, k_inv, k_out = {}, {}, {}, {}
        for cc in order:
            cum = cums[cc]
            e_cum = jnp.exp(cum)
            e_tot[cc] = e_cum[tot_row:tot_row + 1]
            q_in[cc] = q_ref[rows_of(cc), :] * e_cum.astype(BF16)
            k = k_ref[rows_of(cc), :]
            k_inv[cc] = k * jnp.exp(-cum).astype(BF16)
            tot = jnp.broadcast_to(cum[tot_row:tot_row + 1], (c, width))
            k_out[cc] = k * jnp.exp(tot - cum).astype(BF16)
        pair = {(cc, hh): _dot_nt(q_in[cc][:, lanes_of(hh)], k_inv[cc][:, lanes_of(hh)]) for cc, hh in units}
        upd = {(cc, hh): _dot_tn(v_ref[rows_of(cc), lanes_of(hh)], k_out[cc][:, lanes_of(hh)])
               for cc, hh in units}
        intra = {(cc, hh): jnp.dot(jnp.where(pair_mask, pair[cc, hh], 0.0).astype(BF16),
                                   v_ref[rows_of(cc), lanes_of(hh)], preferred_element_type=F32)
                 for cc, hh in units}
        states = {}
        for hh in range(heads):
            st = s_ref[hh]
            for cc in order:
                states[cc, hh] = st.astype(BF16)
                st = st * e_tot[cc][:, lanes_of(hh)] + upd[cc, hh]
            s_ref[hh] = st
        for cc, hh in units:
            o = _dot_nt(q_in[cc][:, lanes_of(hh)], states[cc, hh]) + intra[cc, hh]
            o_ref[rows_of(cc), lanes_of(hh)] = o.astype(o_ref.dtype)

    @pl.when(jnp.logical_not(safe))
    def _():
        w = w_ref[...]
        masks = [m_ref[i] > 0.5 for i in range(m_ref.shape[0])]

        @pl.loop(0, n_chunks)
        def _(ci):
            cc = (n_chunks - 1 - ci) if reverse else ci
            rows = pl.ds(pl.multiple_of(cc * c, c), c)
            sums = jnp.dot(w, g2_ref[cc], preferred_element_type=F32)
            cum = sums[0:c]
            exps = [cum, jnp.broadcast_to(cum[tot_row:tot_row + 1], (c, width)) - cum]
            b = c
            while b >= SCAN_SMALL_BLOCK:
                off = b // 2 if reverse else b // 2 - 1
                ref = jnp.concatenate(
                    [jnp.broadcast_to(cum[a + off:a + off + 1], (b, width)) for a in range(0, c, b)], axis=0)
                exps.append(-jnp.abs(cum - ref))
                b //= 2
            for i in range(n_small):
                exps.append(sums[(1 + i) * c:(2 + i) * c])
            e_f32 = jnp.exp(exps[0])
            e_tot = e_f32[tot_row:tot_row + 1]
            e_all = [e_f32.astype(BF16)] + [jnp.exp(x).astype(BF16) for x in exps[1:]]
            for hh in range(heads):
                lanes = slice(hh * HEAD_DIM, (hh + 1) * HEAD_DIM)
                q = q_ref[rows, lanes]
                k = k_ref[rows, lanes]
                a = jnp.where(masks[0], _dot_nt(q, k), 0.0)
                for lv in range(len(masks) - 1):
                    el = e_all[2 + lv][:, lanes]
                    a = jnp.where(masks[1 + lv], _dot_nt(q * el, k * el), a)
                finish(rows, lanes, hh, q * e_all[0][:, lanes], a, v_ref[rows, lanes],
                       k * e_all[1][:, lanes], e_tot)

    @pl.when(j == pl.num_programs(1) - 1)
    def _():
        sf_ref[...] = s_ref[...]


def _scan(q, k, v, g2, safe_tiles, s0, reverse):
    t, d = q.shape
    c = SCAN_CHUNK
    n_heads = d // HEAD_DIM
    hb = min(SCAN_HEADS, n_heads)
    tb = min(SCAN_BLOCK, t)
    nb = t // tb
    width = hb * HEAD_DIM
    w, m = _scan_tables(reverse)
    safe = safe_tiles[:, 0, :n_heads // hb]
    tok = (lambda h, j, s: (nb - 1 - j, h)) if reverse else (lambda h, j, s: (j, h))
    tok3 = (lambda h, j, s: (nb - 1 - j, 0, h)) if reverse else (lambda h, j, s: (j, 0, h))
    seq_spec = pl.BlockSpec((tb, width), tok)
    state_spec = pl.BlockSpec((hb, HEAD_DIM, HEAD_DIM), lambda h, j, s: (h, 0, 0))
    return pl.pallas_call(
        functools.partial(_scan_kernel, reverse=reverse, heads=hb, n_chunks=tb // c),
        grid_spec=pltpu.PrefetchScalarGridSpec(
            num_scalar_prefetch=1,
            grid=(n_heads // hb, nb),
            in_specs=[seq_spec] * 3 + [
                pl.BlockSpec((tb // c, 2 * c, width), tok3),
                state_spec,
                pl.BlockSpec(w.shape, lambda h, j, s: (0, 0)),
                pl.BlockSpec(m.shape, lambda h, j, s: (0, 0, 0)),
            ],
            out_specs=[seq_spec, state_spec],
            scratch_shapes=[pltpu.VMEM((hb, HEAD_DIM, HEAD_DIM), F32)],
        ),
        out_shape=[jax.ShapeDtypeStruct((t, d), BF16),
                   jax.ShapeDtypeStruct((n_heads, HEAD_DIM, HEAD_DIM), F32)],
        compiler_params=_params("parallel", "arbitrary"),
        name="hgrn_scan_bwd" if reverse else "hgrn_scan_fwd",
    )(safe, q, k, v, g2, s0, w, m)


def _readout_kernel(x_ref, of_ref, ob_ref, og_ref, ng_ref, w_ref, mod_ref, ln_ref, o_ref, *, alpha, n_heads):
    tm = x_ref.shape[0]
    rc = min(PROJ_ROW_CHUNK, tm)

    def matmul(r):
        rows = slice(r * rc, (r + 1) * rc)
        parts = []
        for hh in range(n_heads):
            lanes = slice(hh * HEAD_DIM, (hh + 1) * HEAD_DIM)
            o = of_ref[rows, lanes].astype(F32) + ob_ref[rows, lanes].astype(F32)
            ms = jnp.mean(o * o, axis=-1, keepdims=True)
            nrm = o * lax.rsqrt(ms + RMS_EPS) * ng_ref[...]
            parts.append((nrm * og_ref[rows, lanes].astype(F32)).astype(BF16))
        return jnp.dot(jnp.concatenate(parts, axis=1), w_ref[...], preferred_element_type=F32)

    y = matmul(0)
    for r in range(tm // rc):
        rows = slice(r * rc, (r + 1) * rc)
        y_next = matmul(r + 1) if r + 1 < tm // rc else None
        o_ref[rows, :] = _ln_residual(x_ref[rows, :], y, mod_ref, ln_ref, alpha)
        y = y_next


def _readout(x, o_f, o_b, og, norm_g, w_out, mod, ln, alpha, tm):
    t, d = x.shape
    tm = min(tm, t)
    tile = pl.BlockSpec((tm, d), lambda i: (i, 0))
    return pl.pallas_call(
        functools.partial(_readout_kernel, alpha=alpha, n_heads=d // HEAD_DIM),
        grid=(t // tm,),
        in_specs=[tile, tile, tile, tile,
                  pl.BlockSpec((1, HEAD_DIM), lambda i: (0, 0)),
                  pl.BlockSpec((d, d), lambda i: (0, 0)),
                  pl.BlockSpec((3, d), lambda i: (0, 0)),
                  pl.BlockSpec((2, d), lambda i: (0, 0))],
        out_specs=tile,
        out_shape=jax.ShapeDtypeStruct((t, d), F32),
        compiler_params=_params("parallel"),
        name="hgrn_readout",
    )(x, o_f, o_b, og, norm_g, w_out, mod, ln)


def kernel(x, c, ctx, c_ctx, mod_w, mod_b, ln_g, ln_b, ffn_w_in, ffn_w_out, pool_w, pool_scale,
           hgrn_w_in, hgrn_lb, hgrn_norm_g, hgrn_w_out):
    batch, seq, d = x.shape
    assert batch == 1 and c.shape[0] == 1
    depth = mod_w.shape[0]
    alpha = float((2 * depth) ** 0.25)
    n_sub = mod_w.shape[2] // (3 * d)
    n_heads = d // HEAD_DIM
    tm = 512

    xs = x.reshape(seq, d)
    cs = ctx.reshape(ctx.shape[1], d)
    cb = jnp.broadcast_to(jnp.stack([c.reshape(d), c_ctx.reshape(d)])[:, :, None], (2, d, LANES))
    mods = _modulation(cb, mod_w, mod_b).reshape(depth, 2, n_sub, 3, d)
    ln = jnp.stack([ln_g, ln_b], axis=2)
    w_in_b = ffn_w_in.astype(BF16)
    w_out_b = ffn_w_out.astype(BF16)
    w_hgrn_b = hgrn_w_in.astype(BF16)

    for i in range(depth):
        last = i == depth - 1
        kind = i % 2
        jm = i // 2
        ctx_used = (not last) or kind == 1
        mx, mc = mods[i, 0], mods[i, 1]

        xs = _ffn(xs, mx[0], ln[i, 0], w_in_b, w_out_b, i, 0, alpha, FFN_ROWS)
        if ctx_used:
            cs = _ffn(cs, mc[0], ln[i, 0], w_in_b, w_out_b, i, 0, alpha, tm)

        if kind == 0:
            pw = pool_w[jm].astype(BF16)
            ps = pool_scale[jm].reshape(1, d)
            xs_new = _pool(xs, mx[1], ln[i, 1], pw, ps, alpha, GRID_W, 8)
            if not last:
                cs = _pool(cs, mc[1], ln[i, 1], pw, ps, alpha, cs.shape[0], 1)
            xs = xs_new
        else:
            wh = (w_hgrn_b, jm)
            q_scale = HEAD_DIM ** -0.5
            zero = jnp.zeros((n_heads, HEAD_DIM, HEAD_DIM), F32)
            vc = _proj(cs, mc[1], wh, 1, "linear", tm)
            kfc, gfc, safe_fc = _proj(cs, mc[1], wh, 2, "gate", tm, hgrn_lb[0], i)
            kbc, gbc, safe_bc = _proj(cs, mc[1], wh, 3, "gate", tm, hgrn_lb[1], i)
            if not last:
                qc = _proj(cs, mc[1], wh, 0, "silu", tm, q_scale=q_scale)
                ogc = _proj(cs, mc[1], wh, 4, "silu", tm)
            else:
                qc = vc
            ocf, s_f = _scan(qc, kfc, vc, gfc, safe_fc, zero, False)
            ocb, s_b = _scan(qc, kbc, vc, gbc, safe_bc, zero, True)
            q = _proj(xs, mx[1], wh, 0, "silu", tm, q_scale=q_scale)
            v = _proj(xs, mx[1], wh, 1, "linear", tm)
            kf, gf, safe_f = _proj(xs, mx[1], wh, 2, "gate", tm, hgrn_lb[0], i)
            kb, gb, safe_b = _proj(xs, mx[1], wh, 3, "gate", tm, hgrn_lb[1], i)
            og = _proj(xs, mx[1], wh, 4, "silu", tm)
            o_f, _ = _scan(q, kf, v, gf, safe_f, s_f, False)
            o_b, _ = _scan(q, kb, v, gb, safe_b, s_b, True)
            ng = hgrn_norm_g[jm].reshape(1, HEAD_DIM)
            wo = hgrn_w_out[jm].astype(BF16)
            xs_new = _readout(xs, o_f, o_b, og, ng, wo, mx[1], ln[i, 1], alpha, tm)
            if not last:
                cs = _readout(cs, ocf, ocb, ogc, ng, wo, mc[1], ln[i, 1], alpha, tm)
            xs = xs_new

        xs = _ffn(xs, mx[2], ln[i, 2], w_in_b, w_out_b, i, 1, alpha, FFN_ROWS)
        if not last:
            cs = _ffn(cs, mc[2], ln[i, 2], w_in_b, w_out_b, i, 1, alpha, tm)

    return xs.reshape(batch, seq, d)
```

```python
import functools

import numpy as np
import jax
import jax.numpy as jnp
from jax import lax
from jax.experimental import pallas as pl
from jax.experimental.pallas import tpu as pltpu

GRID_W = 64
POOL_WINDOWS = (2, 4, 8, 16)
HEAD_DIM = 128
SCAN_CHUNK = 128
PROJ_ROW_CHUNK = 256
SCAN_BLOCK = 512
SCAN_HEADS = 4
SCAN_SMALL_BLOCK = 8
SAFE_LOG_DECAY = -80.0
NEG_LOG2_E = -1.4426950408889634
LN_EPS = 1e-5
RMS_EPS = 1e-6
LANES = 128
VMEM_LIMIT_BYTES = 60 * 1024 * 1024
FFN_ROWS = 1024
FFN_OUT_BLOCK = 512

F32 = jnp.float32
BF16 = jnp.bfloat16


def _params(*sem):
    return pltpu.CompilerParams(dimension_semantics=sem, vmem_limit_bytes=VMEM_LIMIT_BYTES)


def _largest_tile(n, cap):
    best = LANES
    for t in range(LANES, min(n, cap) + 1, LANES):
        if n % t == 0:
            best = t
    assert n % best == 0
    return best


def _mod_in(x, mod_ref):
    return x * (1.0 + mod_ref[1:2, :]) + mod_ref[0:1, :]


def _ln_residual(x, y, mod_ref, ln_ref, alpha):
    z = alpha * x + mod_ref[2:3, :] * y
    mu = jnp.mean(z, axis=-1, keepdims=True)
    zc = z - mu
    var = jnp.mean(zc * zc, axis=-1, keepdims=True)
    return zc * lax.rsqrt(var + LN_EPS) * ln_ref[0:1, :] + ln_ref[1:2, :]


def _sigmoid(a):
    return 1.0 / (1.0 + jnp.exp(-a))


def _mod_kernel(cb_ref, w_ref, b_ref, o_ref, *, tn):
    for who in range(2):
        cv = cb_ref[who]
        s = cv * _sigmoid(cv)
        for l in range(tn // LANES):
            sl = slice(l * LANES, (l + 1) * LANES)
            acc = jnp.sum(w_ref[0, :, sl] * s, axis=0, keepdims=True)
            o_ref[0, who:who + 1, sl] = acc + b_ref[0, :, sl]


def _modulation(cb, mod_w, mod_b):
    depth, d, n = mod_w.shape
    tn = _largest_tile(n, 1024)
    return pl.pallas_call(
        functools.partial(_mod_kernel, tn=tn),
        grid=(depth, n // tn),
        in_specs=[
            pl.BlockSpec((2, d, LANES), lambda i, j: (0, 0, 0)),
            pl.BlockSpec((1, d, tn), lambda i, j: (i, 0, j)),
            pl.BlockSpec((1, 1, tn), lambda i, j: (i, 0, j)),
        ],
        out_specs=pl.BlockSpec((1, 2, tn), lambda i, j: (i, 0, j)),
        out_shape=jax.ShapeDtypeStruct((depth, 2, n), F32),
        compiler_params=_params("arbitrary", "arbitrary"),
        name="modulation",
    )(cb, mod_w, mod_b.reshape(depth, 1, n))


def _ffn_kernel(x_ref, mod_ref, ln_ref, wa_ref, wu_ref, wo_ref, o_ref, h_ref, *, alpha, nf):
    f = pl.program_id(1)
    tm = x_ref.shape[0]

    def prologue(rows):
        h_ref[rows, :] = _mod_in(x_ref[rows, :], mod_ref).astype(BF16)

    def step(rows, first):
        h = h_ref[rows, :]
        a = jnp.dot(h, wa_ref[...], preferred_element_type=F32)
        u = jnp.dot(h, wu_ref[...], preferred_element_type=F32)
        g = (a * _sigmoid(a) * u).astype(BF16)
        for n in range(o_ref.shape[1] // FFN_OUT_BLOCK):
            cols = slice(n * FFN_OUT_BLOCK, (n + 1) * FFN_OUT_BLOCK)
            part = jnp.dot(g, wo_ref[:, cols], preferred_element_type=F32)
            if first:
                o_ref[rows, cols] = part
            else:
                o_ref[rows, cols] += part

    def epilogue(rows):
        o_ref[rows, :] = _ln_residual(x_ref[rows, :], 0.5 * o_ref[rows, :], mod_ref, ln_ref, alpha)

    halves = [slice(0, tm // 2), slice(tm // 2, tm)] if tm >= 512 else [slice(0, tm)]

    def edge_step(first, last):
        if first:
            prologue(halves[0])
        for i, rows in enumerate(halves):
            step(rows, first)
            if first and i + 1 < len(halves):
                prologue(halves[i + 1])
        if last:
            for rows in halves:
                epilogue(rows)

    if nf == 1:
        edge_step(True, True)
    else:
        pl.when(f == 0)(lambda: edge_step(True, False))
        pl.when(jnp.logical_and(f > 0, f < nf - 1))(lambda: step(slice(0, tm), False))
        pl.when(f == nf - 1)(lambda: edge_step(False, True))


def _ffn(x, mod, ln, w_in, w_out, layer, half, alpha, tm):
    t, d = x.shape
    d_ff = w_out.shape[2]
    tm = min(tm, t)
    tf = _largest_tile(d_ff, 512)
    nf = d_ff // tf
    return pl.pallas_call(
        functools.partial(_ffn_kernel, alpha=alpha, nf=nf),
        grid=(t // tm, nf),
        in_specs=[
            pl.BlockSpec((tm, d), lambda i, f: (i, 0)),
            pl.BlockSpec((3, d), lambda i, f: (0, 0)),
            pl.BlockSpec((2, d), lambda i, f: (0, 0)),
            pl.BlockSpec((None, None, d, tf), lambda i, f: (layer, half, 0, f)),
            pl.BlockSpec((None, None, d, tf), lambda i, f: (layer, half, 0, nf + f)),
            pl.BlockSpec((None, None, tf, d), lambda i, f: (layer, half, f, 0)),
        ],
        out_specs=pl.BlockSpec((tm, d), lambda i, f: (i, 0)),
        out_shape=jax.ShapeDtypeStruct((t, d), F32),
        scratch_shapes=[pltpu.VMEM((tm, d), BF16)],
        compiler_params=_params("parallel", "arbitrary"),
        name="swiglu_half_step",
    )(x, mod, ln, w_in, w_in, w_out)


def _pool_kernel(*refs, alpha, row_len, n_rows, rows_per_tile, halo_rows, pool_g):
    if halo_rows:
        x_ref, xp_ref, xn_ref, mod_ref, ln_ref, w_ref, sc_ref, o_ref = refs
    else:
        x_ref, mod_ref, ln_ref, w_ref, sc_ref, o_ref = refs
    i = pl.program_id(0)
    tm = rows_per_tile * row_len
    tok = lax.broadcasted_iota(jnp.int32, (tm, LANES), 0)
    col = tok % row_len
    row = i * rows_per_tile + tok // row_len
    reps = pool_g // LANES

    def widen(a):
        return jnp.concatenate([a] * reps, axis=1) if reps > 1 else a

    x = x_ref[...]
    ys = []
    for gi, win in enumerate(POOL_WINDOWS):
        lanes = slice(gi * pool_g, (gi + 1) * pool_g)
        before = win // 2
        h = x[:, lanes] * (1.0 + mod_ref[1:2, lanes]) + mod_ref[0:1, lanes]
        if halo_rows:
            nh = halo_rows * row_len
            keep_p = (i > 0).astype(F32)
            keep_n = (i < pl.num_programs(0) - 1).astype(F32)
            hp = (xp_ref[:, lanes] * (1.0 + mod_ref[1:2, lanes]) + mod_ref[0:1, lanes]) * keep_p
            hn = (xn_ref[:, lanes] * (1.0 + mod_ref[1:2, lanes]) + mod_ref[0:1, lanes]) * keep_n
            buf = jnp.concatenate([hp, h, hn], axis=0)
            vs = None
            for dr in range(-before, win - before):
                start = nh + dr * row_len
                term = buf[start:start + tm, :]
                vs = term if vs is None else vs + term
            lo = jnp.maximum(row - before, 0)
            hi = jnp.minimum(row - before + win, n_rows)
            cnt = (hi - lo).astype(F32)
        else:
            vs = h
            cnt = jnp.ones((tm, LANES), F32)
        hs = vs
        for dc in range(-before, win - before):
            if dc == 0:
                continue
            valid = jnp.logical_and(col + dc >= 0, col + dc < row_len)
            shifted = pltpu.roll(vs, (-dc) % tm, 0)
            hs = hs + jnp.where(widen(valid), shifted, 0.0)
        lo = jnp.maximum(col - before, 0)
        hi = jnp.minimum(col - before + win, row_len)
        cnt = cnt * (hi - lo).astype(F32)
        p = (hs / widen(cnt) - h).astype(BF16)
        ys.append(jnp.dot(p, w_ref[gi], preferred_element_type=F32))
    y = jnp.concatenate(ys, axis=1) * sc_ref[...]
    o_ref[...] = _ln_residual(x, y, mod_ref, ln_ref, alpha)


def _pool(x, mod, ln, w, scale, alpha, row_len, rows_per_tile):
    t, d = x.shape
    n_rows = t // row_len
    groups = len(POOL_WINDOWS)
    pool_g = d // groups
    halo_rows = 0 if n_rows == 1 else max(POOL_WINDOWS) // 2
    rows_per_tile = min(rows_per_tile, n_rows)
    tm = rows_per_tile * row_len
    nt = n_rows // rows_per_tile
    common = [
        pl.BlockSpec((3, d), lambda i: (0, 0)),
        pl.BlockSpec((2, d), lambda i: (0, 0)),
        pl.BlockSpec((groups, pool_g, pool_g), lambda i: (0, 0, 0)),
        pl.BlockSpec((1, d), lambda i: (0, 0)),
    ]
    if halo_rows:
        assert rows_per_tile % halo_rows == 0
        nh = halo_rows * row_len
        per = tm // nh
        last = t // nh - 1
        in_specs = [
            pl.BlockSpec((tm, d), lambda i: (i, 0)),
            pl.BlockSpec((nh, d), lambda i: (jnp.maximum(i * per - 1, 0), 0)),
            pl.BlockSpec((nh, d), lambda i: (jnp.minimum((i + 1) * per, last), 0)),
        ] + common
        args = (x, x, x, mod, ln, w, scale)
    else:
        in_specs = [pl.BlockSpec((tm, d), lambda i: (i, 0))] + common
        args = (x, mod, ln, w, scale)
    return pl.pallas_call(
        functools.partial(_pool_kernel, alpha=alpha, row_len=row_len, n_rows=n_rows,
                          rows_per_tile=rows_per_tile, halo_rows=halo_rows, pool_g=pool_g),
        grid=(nt,),
        in_specs=in_specs,
        out_specs=pl.BlockSpec((tm, d), lambda i: (i, 0)),
        out_shape=jax.ShapeDtypeStruct((t, d), F32),
        compiler_params=_params("parallel"),
        name="pool_mixer",
    )(*args)


def _proj_kernel(*refs, mode, layer, q_scale):
    if mode == "gate":
        x_ref, mod_ref, w_ref, lb_ref, k_ref, g2_ref, safe_ref = refs
    else:
        x_ref, mod_ref, w_ref, o_ref = refs
    tm = x_ref.shape[0]
    rc = min(PROJ_ROW_CHUNK, tm)
    c = SCAN_CHUNK

    if mode == "gate":
        raw = lb_ref[...]
        e = jnp.exp(raw - jnp.max(raw, axis=0, keepdims=True))
        p = e / jnp.sum(e, axis=0, keepdims=True)
        lb = jnp.sum(p[1:layer + 1, :], axis=0, keepdims=True) if layer > 0 else jnp.zeros_like(p[0:1, :])
        log_lb = jnp.log(lb)
        log_1m_lb = jnp.log(1.0 - lb)

    def matmul(r):
        rows = slice(r * rc, (r + 1) * rc)
        h = _mod_in(x_ref[rows, :], mod_ref).astype(BF16)
        return jnp.dot(h, w_ref[...], preferred_element_type=F32)

    def epilogue(r, z):
        rows = slice(r * rc, (r + 1) * rc)
        if mode == "silu":
            o_ref[rows, :] = (z * _sigmoid(z) * q_scale).astype(BF16)
        elif mode == "linear":
            o_ref[rows, :] = z.astype(BF16)
        else:
            ez = jnp.exp2(jnp.abs(z) * NEG_LOG2_E)
            one_ez = 1.0 + ez
            x2 = log_1m_lb + (jnp.minimum(z, 0.0) - jnp.log(one_ez))
            g = jnp.maximum(log_lb, x2) + jnp.log(1.0 + jnp.exp2(jnp.abs(log_lb - x2) * NEG_LOG2_E))
            inv = 1.0 / one_ez
            k = (1.0 - lb) * jnp.where(z >= 0.0, ez * inv, inv)
            gh = g.astype(BF16)
            gl = (g - gh.astype(F32)).astype(BF16)
            k_ref[rows, :] = k.astype(BF16)
            chunks = slice(r * (rc // c), (r + 1) * (rc // c))
            g2_ref[chunks, 0:c, :] = gh.reshape(rc // c, c, g.shape[1])
            g2_ref[chunks, c:2 * c, :] = gl.reshape(rc // c, c, g.shape[1])
            return jnp.min(jnp.sum(g.reshape(rc // c, c, g.shape[1]), axis=1), axis=0, keepdims=True)

    z = matmul(0)
    lowest = None
    for r in range(tm // rc):
        z_next = matmul(r + 1) if r + 1 < tm // rc else None
        low = epilogue(r, z)
        if mode == "gate":
            lowest = low if lowest is None else jnp.minimum(lowest, low)
        z = z_next

    if mode == "gate":
        width = SCAN_HEADS * HEAD_DIM
        lane = lax.broadcasted_iota(jnp.int32, safe_ref.shape[1:], 1)
        flags = jnp.zeros(safe_ref.shape[1:], jnp.int32)
        for b in range(lowest.shape[1] // width):
            ok = jnp.min(lowest[:, b * width:(b + 1) * width], axis=1, keepdims=True) >= SAFE_LOG_DECAY
            flags = jnp.where(jnp.logical_and(lane == b, ok), 1, flags)
        safe_ref[0] = flags


def _proj(x, mod, w, col, mode, tm, lb_raw=None, layer=0, q_scale=1.0):
    w, w_layer = w
    t, d = x.shape
    n = d
    tm = min(tm, t)
    in_specs = [
        pl.BlockSpec((tm, d), lambda i: (i, 0)),
        pl.BlockSpec((3, d), lambda i: (0, 0)),
        pl.BlockSpec((None, d, n), lambda i: (w_layer, 0, col)),
    ]
    out_spec = pl.BlockSpec((tm, n), lambda i: (i, 0))
    out_sds = jax.ShapeDtypeStruct((t, n), BF16)
    args = [x, mod, w]
    if mode == "gate":
        in_specs.append(pl.BlockSpec(lb_raw.shape, lambda i: (0, 0)))
        args.append(lb_raw)
        c = SCAN_CHUNK
        assert tm == min(SCAN_BLOCK, t)
        out_specs = [out_spec, pl.BlockSpec((tm // c, 2 * c, n), lambda i: (i, 0, 0)),
                     pl.BlockSpec((1, 8, LANES), lambda i: (i, 0, 0))]
        out_shape = [out_sds, jax.ShapeDtypeStruct((t // c, 2 * c, n), BF16),
                     jax.ShapeDtypeStruct((t // tm, 8, LANES), jnp.int32)]
    else:
        out_specs, out_shape = out_spec, out_sds
    return pl.pallas_call(
        functools.partial(_proj_kernel, mode=mode, layer=layer, q_scale=q_scale),
        grid=(t // tm,),
        in_specs=in_specs,
        out_specs=out_specs,
        out_shape=out_shape,
        compiler_params=_params("parallel"),
        name="hgrn_proj_" + mode,
    )(*args)


def _scan_tables(reverse):
    c = SCAN_CHUNK
    pos = np.arange(c)[::-1] if reverse else np.arange(c)
    pt, pr = pos[:, None], pos[None, :]
    blocks = [pr <= pt]
    masks = [pt == pr]
    b = c
    while b >= 2:
        mid = (pt // b) * b + b // 2
        late = pt >= mid
        if b < SCAN_SMALL_BLOCK:
            blocks.append(np.where(late, (pr >= mid) & (pr <= pt), (pr > pt) & (pr < mid)))
        mid_s = (pr // b) * b + b // 2
        masks.append((pt // b == pr // b) & late & (pr < mid_s))
        b //= 2
    w = np.concatenate(blocks, axis=0).astype(np.float32)
    w = np.concatenate([w, w], axis=1)
    m = np.stack(masks, axis=0).astype(np.float32)
    return jnp.asarray(w, BF16), jnp.asarray(m, F32)


def _dot_nt(a, b):
    return lax.dot_general(a, b, (((1,), (1,)), ((), ())), preferred_element_type=F32)


def _dot_tn(a, b):
    return lax.dot_general(a, b, (((0,), (0,)), ((), ())), preferred_element_type=F32)


def _scan_kernel(safe_ref, q_ref, k_ref, v_ref, g2_ref, s0_ref, w_ref, m_ref, o_ref, sf_ref, s_ref,
                 *, reverse, heads, n_chunks):
    j = pl.program_id(1)
    c = SCAN_CHUNK
    width = heads * HEAD_DIM
    tot_row = 0 if reverse else c - 1

    @pl.when(j == 0)
    def _():
        s_ref[...] = s0_ref[...]

    n_small = w_ref.shape[0] // c - 1
    order = range(n_chunks - 1, -1, -1) if reverse else range(n_chunks)

    def finish(rows, lanes, hh, q_in, a, v, k_out, e_tot):
        st = s_ref[hh]
        o = _dot_nt(q_in, st.astype(BF16)) + jnp.dot(a.astype(BF16), v, preferred_element_type=F32)
        o_ref[rows, lanes] = o.astype(o_ref.dtype)
        s_ref[hh] = st * e_tot[:, lanes] + _dot_tn(v, k_out)

    block = (pl.num_programs(1) - 1 - j) if reverse else j
    safe = safe_ref[block, pl.program_id(0)] == 1

    @pl.when(safe)
    def _():
        pair_mask = jnp.sum(m_ref[...], axis=0) > 0.5
        w_cum = w_ref[0:c, :]
        units = [(cc, hh) for cc in order for hh in range(heads)]
        rows_of = lambda cc: slice(cc * c, (cc + 1) * c)
        lanes_of = lambda hh: slice(hh * HEAD_DIM, (hh + 1) * HEAD_DIM)
        cums = {cc: jnp.dot(w_cum, g2_ref[cc], preferred_element_type=F32) for cc in order}
        e_tot, q_in, k_inv, k_out = {}, {}, {}, {}
        for cc in order:
            cum = cums[cc]
            e_cum = jnp.exp(cum)
            e_tot[cc] = e_cum[tot_row:tot_row + 1]
            q_in[cc] = q_ref[rows_of(cc), :] * e_cum.astype(BF16)
            k = k_ref[rows_of(cc), :]
            k_inv[cc] = k * jnp.exp(-cum).astype(BF16)
            tot = jnp.broadcast_to(cum[tot_row:tot_row + 1], (c, width))
            k_out[cc] = k * jnp.exp(tot - cum).astype(BF16)
        pair = {(cc, hh): _dot_nt(q_in[cc][:, lanes_of(hh)], k_inv[cc][:, lanes_of(hh)]) for cc, hh in units}
        upd = {(cc, hh): _dot_tn(v_ref[rows_of(cc), lanes_of(hh)], k_out[cc][:, lanes_of(hh)])
               for cc, hh in units}
        intra = {(cc, hh): jnp.dot(jnp.where(pair_mask, pair[cc, hh], 0.0).astype(BF16),
                                   v_ref[rows_of(cc), lanes_of(hh)], preferred_element_type=F32)
                 for cc, hh in units}
        states = {}
        for hh in range(heads):
            st = s_ref[hh]
            for cc in order:
                states[cc, hh] = st.astype(BF16)
                st = st * e_tot[cc][:, lanes_of(hh)] + upd[cc, hh]
            s_ref[hh] = st
        for cc, hh in units:
            o = _dot_nt(q_in[cc][:, lanes_of(hh)], states[cc, hh]) + intra[cc, hh]
            o_ref[rows_of(cc), lanes_of(hh)] = o.astype(o_ref.dtype)

    @pl.when(jnp.logical_not(safe))
    def _():
        w = w_ref[...]
        masks = [m_ref[i] > 0.5 for i in range(m_ref.shape[0])]

        @pl.loop(0, n_chunks)
        def _(ci):
            cc = (n_chunks - 1 - ci) if reverse else ci
            rows = pl.ds(pl.multiple_of(cc * c, c), c)
            sums = jnp.dot(w, g2_ref[cc], preferred_element_type=F32)
            cum = sums[0:c]
            exps = [cum, jnp.broadcast_to(cum[tot_row:tot_row + 1], (c, width)) - cum]
            b = c
            while b >= SCAN_SMALL_BLOCK:
                off = b // 2 if reverse else b // 2 - 1
                ref = jnp.concatenate(
                    [jnp.broadcast_to(cum[a + off:a + off + 1], (b, width)) for a in range(0, c, b)], axis=0)
                exps.append(-jnp.abs(cum - ref))
                b //= 2
            for i in range(n_small):
                exps.append(sums[(1 + i) * c:(2 + i) * c])
            e_f32 = jnp.exp(exps[0])
            e_tot = e_f32[tot_row:tot_row + 1]
            e_all = [e_f32.astype(BF16)] + [jnp.exp(x).astype(BF16) for x in exps[1:]]
            for hh in range(heads):
                lanes = slice(hh * HEAD_DIM, (hh + 1) * HEAD_DIM)
                q = q_ref[rows, lanes]
                k = k_ref[rows, lanes]
                a = jnp.where(masks[0], _dot_nt(q, k), 0.0)
                for lv in range(len(masks) - 1):
                    el = e_all[2 + lv][:, lanes]
                    a = jnp.where(masks[1 + lv], _dot_nt(q * el, k * el), a)
                finish(rows, lanes, hh, q * e_all[0][:, lanes], a, v_ref[rows, lanes],
                       k * e_all[1][:, lanes], e_tot)

    @pl.when(j == pl.num_programs(1) - 1)
    def _():
        sf_ref[...] = s_ref[...]


def _scan(q, k, v, g2, safe_tiles, s0, reverse):
    t, d = q.shape
    c = SCAN_CHUNK
    n_heads = d // HEAD_DIM
    hb = min(SCAN_HEADS, n_heads)
    tb = min(SCAN_BLOCK, t)
    nb = t // tb
    width = hb * HEAD_DIM
    w, m = _scan_tables(reverse)
    safe = safe_tiles[:, 0, :n_heads // hb]
    tok = (lambda h, j, s: (nb - 1 - j, h)) if reverse else (lambda h, j, s: (j, h))
    tok3 = (lambda h, j, s: (nb - 1 - j, 0, h)) if reverse else (lambda h, j, s: (j, 0, h))
    seq_spec = pl.BlockSpec((tb, width), tok)
    state_spec = pl.BlockSpec((hb, HEAD_DIM, HEAD_DIM), lambda h, j, s: (h, 0, 0))
    return pl.pallas_call(
        functools.partial(_scan_kernel, reverse=reverse, heads=hb, n_chunks=tb // c),
        grid_spec=pltpu.PrefetchScalarGridSpec(
            num_scalar_prefetch=1,
            grid=(n_heads // hb, nb),
            in_specs=[seq_spec] * 3 + [
                pl.BlockSpec((tb // c, 2 * c, width), tok3),
                state_spec,
                pl.BlockSpec(w.shape, lambda h, j, s: (0, 0)),
                pl.BlockSpec(m.shape, lambda h, j, s: (0, 0, 0)),
            ],
            out_specs=[seq_spec, state_spec],
            scratch_shapes=[pltpu.VMEM((hb, HEAD_DIM, HEAD_DIM), F32)],
        ),
        out_shape=[jax.ShapeDtypeStruct((t, d), BF16),
                   jax.ShapeDtypeStruct((n_heads, HEAD_DIM, HEAD_DIM), F32)],
        compiler_params=_params("parallel", "arbitrary"),
        name="hgrn_scan_bwd" if reverse else "hgrn_scan_fwd",
    )(safe, q, k, v, g2, s0, w, m)


def _readout_kernel(x_ref, of_ref, ob_ref, og_ref, ng_ref, w_ref, mod_ref, ln_ref, o_ref, *, alpha, n_heads):
    tm = x_ref.shape[0]
    rc = min(PROJ_ROW_CHUNK, tm)

    def matmul(r):
        rows = slice(r * rc, (r + 1) * rc)
        parts = []
        for hh in range(n_heads):
            lanes = slice(hh * HEAD_DIM, (hh + 1) * HEAD_DIM)
            o = of_ref[rows, lanes].astype(F32) + ob_ref[rows, lanes].astype(F32)
            ms = jnp.mean(o * o, axis=-1, keepdims=True)
            nrm = o * lax.rsqrt(ms + RMS_EPS) * ng_ref[...]
            parts.append((nrm * og_ref[rows, lanes].astype(F32)).astype(BF16))
        return jnp.dot(jnp.concatenate(parts, axis=1), w_ref[...], preferred_element_type=F32)

    y = matmul(0)
    for r in range(tm // rc):
        rows = slice(r * rc, (r + 1) * rc)
        y_next = matmul(r + 1) if r + 1 < tm // rc else None
        o_ref[rows, :] = _ln_residual(x_ref[rows, :], y, mod_ref, ln_ref, alpha)
        y = y_next


def _readout(x, o_f, o_b, og, norm_g, w_out, mod, ln, alpha, tm):
    t, d = x.shape
    tm = min(tm, t)
    tile = pl.BlockSpec((tm, d), lambda i: (i, 0))
    return pl.pallas_call(
        functools.partial(_readout_kernel, alpha=alpha, n_heads=d // HEAD_DIM),
        grid=(t // tm,),
        in_specs=[tile, tile, tile, tile,
                  pl.BlockSpec((1, HEAD_DIM), lambda i: (0, 0)),
                  pl.BlockSpec((d, d), lambda i: (0, 0)),
                  pl.BlockSpec((3, d), lambda i: (0, 0)),
                  pl.BlockSpec((2, d), lambda i: (0, 0))],
        out_specs=tile,
        out_shape=jax.ShapeDtypeStruct((t, d), F32),
        compiler_params=_params("parallel"),
        name="hgrn_readout",
    )(x, o_f, o_b, og, norm_g, w_out, mod, ln)


def kernel(x, c, ctx, c_ctx, mod_w, mod_b, ln_g, ln_b, ffn_w_in, ffn_w_out, pool_w, pool_scale,
           hgrn_w_in, hgrn_lb, hgrn_norm_g, hgrn_w_out):
    batch, seq, d = x.shape
    assert batch == 1 and c.shape[0] == 1
    depth = mod_w.shape[0]
    alpha = float((2 * depth) ** 0.25)
    n_sub = mod_w.shape[2] // (3 * d)
    n_heads = d // HEAD_DIM
    tm = 512

    xs = x.reshape(seq, d)
    cs = ctx.reshape(ctx.shape[1], d)
    cb = jnp.broadcast_to(jnp.stack([c.reshape(d), c_ctx.reshape(d)])[:, :, None], (2, d, LANES))
    mods = _modulation(cb, mod_w, mod_b).reshape(depth, 2, n_sub, 3, d)
    ln = jnp.stack([ln_g, ln_b], axis=2)
    w_in_b = ffn_w_in.astype(BF16)
    w_out_b = ffn_w_out.astype(BF16)
    w_hgrn_b = hgrn_w_in.astype(BF16)

    for i in range(depth):
        last = i == depth - 1
        kind = i % 2
        jm = i // 2
        ctx_used = (not last) or kind == 1
        mx, mc = mods[i, 0], mods[i, 1]

        xs = _ffn(xs, mx[0], ln[i, 0], w_in_b, w_out_b, i, 0, alpha, FFN_ROWS)
        if ctx_used:
            cs = _ffn(cs, mc[0], ln[i, 0], w_in_b, w_out_b, i, 0, alpha, tm)

        if kind == 0:
            pw = pool_w[jm].astype(BF16)
            ps = pool_scale[jm].reshape(1, d)
            xs_new = _pool(xs, mx[1], ln[i, 1], pw, ps, alpha, GRID_W, 8)
            if not last:
                cs = _pool(cs, mc[1], ln[i, 1], pw, ps, alpha, cs.shape[0], 1)
            xs = xs_new
        else:
            wh = (w_hgrn_b, jm)
            q_scale = HEAD_DIM ** -0.5
            zero = jnp.zeros((n_heads, HEAD_DIM, HEAD_DIM), F32)
            vc = _proj(cs, mc[1], wh, 1, "linear", tm)
            kfc, gfc, safe_fc = _proj(cs, mc[1], wh, 2, "gate", tm, hgrn_lb[0], i)
            kbc, gbc, safe_bc = _proj(cs, mc[1], wh, 3, "gate", tm, hgrn_lb[1], i)
            if not last:
                qc = _proj(cs, mc[1], wh, 0, "silu", tm, q_scale=q_scale)
                ogc = _proj(cs, mc[1], wh, 4, "silu", tm)
            else:
                qc = vc
            ocf, s_f = _scan(qc, kfc, vc, gfc, safe_fc, zero, False)
            ocb, s_b = _scan(qc, kbc, vc, gbc, safe_bc, zero, True)
            q = _proj(xs, mx[1], wh, 0, "silu", tm, q_scale=q_scale)
            v = _proj(xs, mx[1], wh, 1, "linear", tm)
            kf, gf, safe_f = _proj(xs, mx[1], wh, 2, "gate", tm, hgrn_lb[0], i)
            kb, gb, safe_b = _proj(xs, mx[1], wh, 3, "gate", tm, hgrn_lb[1], i)
            og = _proj(xs, mx[1], wh, 4, "silu", tm)
            o_f, _ = _scan(q, kf, v, gf, safe_f, s_f, False)
            o_b, _ = _scan(q, kb, v, gb, safe_b, s_b, True)
            ng = hgrn_norm_g[jm].reshape(1, HEAD_DIM)
            wo = hgrn_w_out[jm].astype(BF16)
            xs_new = _readout(xs, o_f, o_b, og, ng, wo, mx[1], ln[i, 1], alpha, tm)
            if not last:
                cs = _readout(cs, ocf, ocb, ogc, ng, wo, mc[1], ln[i, 1], alpha, tm)
            xs = xs_new

        xs = _ffn(xs, mx[2], ln[i, 2], w_in_b, w_out_b, i, 1, alpha, FFN_ROWS)
        if not last:
            cs = _ffn(cs, mc[2], ln[i, 2], w_in_b, w_out_b, i, 1, alpha, tm)

    return xs.reshape(batch, seq, d)
```

```python
import functools

import numpy as np
import jax
import jax.numpy as jnp
from jax import lax
from jax.experimental import pallas as pl
from jax.experimental.pallas import tpu as pltpu

GRID_W = 64
POOL_WINDOWS = (2, 4, 8, 16)
HEAD_DIM = 128
SCAN_CHUNK = 128
PROJ_ROW_CHUNK = 256
SCAN_BLOCK = 512
SCAN_HEADS = 4
SCAN_SMALL_BLOCK = 8
SAFE_LOG_DECAY = -80.0
NEG_LOG2_E = -1.4426950408889634
LN_EPS = 1e-5
RMS_EPS = 1e-6
LANES = 128
VMEM_LIMIT_BYTES = 60 * 1024 * 1024
FFN_ROWS = 1024
FFN_OUT_BLOCK = 512

F32 = jnp.float32
BF16 = jnp.bfloat16


def _params(*sem):
    return pltpu.CompilerParams(dimension_semantics=sem, vmem_limit_bytes=VMEM_LIMIT_BYTES)


def _largest_tile(n, cap):
    best = LANES
    for t in range(LANES, min(n, cap) + 1, LANES):
        if n % t == 0:
            best = t
    assert n % best == 0
    return best


def _mod_in(x, mod_ref):
    return x * (1.0 + mod_ref[1:2, :]) + mod_ref[0:1, :]


def _ln_residual(x, y, mod_ref, ln_ref, alpha):
    z = alpha * x + mod_ref[2:3, :] * y
    mu = jnp.mean(z, axis=-1, keepdims=True)
    zc = z - mu
    var = jnp.mean(zc * zc, axis=-1, keepdims=True)
    return zc * lax.rsqrt(var + LN_EPS) * ln_ref[0:1, :] + ln_ref[1:2, :]


def _sigmoid(a):
    return 1.0 / (1.0 + jnp.exp(-a))


def _mod_kernel(cb_ref, w_ref, b_ref, o_ref, *, tn):
    for who in range(2):
        cv = cb_ref[who]
        s = cv * _sigmoid(cv)
        for l in range(tn // LANES):
            sl = slice(l * LANES, (l + 1) * LANES)
            acc = jnp.sum(w_ref[0, :, sl] * s, axis=0, keepdims=True)
            o_ref[0, who:who + 1, sl] = acc + b_ref[0, :, sl]


def _modulation(cb, mod_w, mod_b):
    depth, d, n = mod_w.shape
    tn = _largest_tile(n, 2048)
    return pl.pallas_call(
        functools.partial(_mod_kernel, tn=tn),
        grid=(depth, n // tn),
        in_specs=[
            pl.BlockSpec((2, d, LANES), lambda i, j: (0, 0, 0)),
            pl.BlockSpec((1, d, tn), lambda i, j: (i, 0, j)),
            pl.BlockSpec((1, 1, tn), lambda i, j: (i, 0, j)),
        ],
        out_specs=pl.BlockSpec((1, 2, tn), lambda i, j: (i, 0, j)),
        out_shape=jax.ShapeDtypeStruct((depth, 2, n), F32),
        compiler_params=_params("arbitrary", "arbitrary"),
        name="modulation",
    )(cb, mod_w, mod_b.reshape(depth, 1, n))


def _ffn_kernel(x_ref, mod_ref, ln_ref, wa_ref, wu_ref, wo_ref, o_ref, h_ref, *, alpha, nf):
    f = pl.program_id(1)
    tm = x_ref.shape[0]

    def prologue(rows):
        h_ref[rows, :] = _mod_in(x_ref[rows, :], mod_ref).astype(BF16)

    def step(rows, first):
        h = h_ref[rows, :]
        a = jnp.dot(h, wa_ref[...], preferred_element_type=F32)
        u = jnp.dot(h, wu_ref[...], preferred_element_type=F32)
        g = (a * _sigmoid(a) * u).astype(BF16)
        for n in range(o_ref.shape[1] // FFN_OUT_BLOCK):
            cols = slice(n * FFN_OUT_BLOCK, (n + 1) * FFN_OUT_BLOCK)
            part = jnp.dot(g, wo_ref[:, cols], preferred_element_type=F32)
            if first:
                o_ref[rows, cols] = part
            else:
                o_ref[rows, cols] += part

    def epilogue(rows):
        o_ref[rows, :] = _ln_residual(x_ref[rows, :], 0.5 * o_ref[rows, :], mod_ref, ln_ref, alpha)

    halves = [slice(0, tm // 2), slice(tm // 2, tm)] if tm >= 512 else [slice(0, tm)]

    def edge_step(first, last):
        if first:
            prologue(halves[0])
        for i, rows in enumerate(halves):
            step(rows, first)
            if first and i + 1 < len(halves):
                prologue(halves[i + 1])
        if last:
            for rows in halves:
                epilogue(rows)

    if nf == 1:
        edge_step(True, True)
    else:
        pl.when(f == 0)(lambda: edge_step(True, False))
        pl.when(jnp.logical_and(f > 0, f < nf - 1))(lambda: step(slice(0, tm), False))
        pl.when(f == nf - 1)(lambda: edge_step(False, True))


def _ffn(x, mod, ln, w_in, w_out, layer, half, alpha, tm):
    t, d = x.shape
    d_ff = w_out.shape[2]
    tm = min(tm, t)
    tf = _largest_tile(d_ff, 512)
    nf = d_ff // tf
    return pl.pallas_call(
        functools.partial(_ffn_kernel, alpha=alpha, nf=nf),
        grid=(t // tm, nf),
        in_specs=[
            pl.BlockSpec((tm, d), lambda i, f: (i, 0)),
            pl.BlockSpec((3, d), lambda i, f: (0, 0)),
            pl.BlockSpec((2, d), lambda i, f: (0, 0)),
            pl.BlockSpec((None, None, d, tf), lambda i, f: (layer, half, 0, f)),
            pl.BlockSpec((None, None, d, tf), lambda i, f: (layer, half, 0, nf + f)),
            pl.BlockSpec((None, None, tf, d), lambda i, f: (layer, half, f, 0)),
        ],
        out_specs=pl.BlockSpec((tm, d), lambda i, f: (i, 0)),
        out_shape=jax.ShapeDtypeStruct((t, d), F32),
        scratch_shapes=[pltpu.VMEM((tm, d), BF16)],
        compiler_params=_params("parallel", "arbitrary"),
        name="swiglu_half_step",
    )(x, mod, ln, w_in, w_in, w_out)


def _pool_kernel(*refs, alpha, row_len, n_rows, rows_per_tile, halo_rows, pool_g):
    if halo_rows:
        x_ref, xp_ref, xn_ref, mod_ref, ln_ref, w_ref, sc_ref, o_ref = refs
    else:
        x_ref, mod_ref, ln_ref, w_ref, sc_ref, o_ref = refs
    i = pl.program_id(0)
    tm = rows_per_tile * row_len
    tok = lax.broadcasted_iota(jnp.int32, (tm, LANES), 0)
    col = tok % row_len
    row = i * rows_per_tile + tok // row_len
    reps = pool_g // LANES

    def widen(a):
        return jnp.concatenate([a] * reps, axis=1) if reps > 1 else a

    def shifted(a, d):
        return pltpu.roll(a, (-d) % tm, 0)

    half_windows = [w for w in (1, 2, 4, 8) if 2 * w < max(POOL_WINDOWS)]
    fits_after = {w: widen(col + w < row_len) for w in half_windows}
    fits_before = {w: widen(col - w >= 0) for w in half_windows}

    x = x_ref[...]
    ys = []
    for gi, win in enumerate(POOL_WINDOWS):
        lanes = slice(gi * pool_g, (gi + 1) * pool_g)
        before = win // 2
        h = x[:, lanes] * (1.0 + mod_ref[1:2, lanes]) + mod_ref[0:1, lanes]
        if halo_rows:
            nh = halo_rows * row_len
            keep_p = (i > 0).astype(F32)
            keep_n = (i < pl.num_programs(0) - 1).astype(F32)
            above = xp_ref[nh - before * row_len:nh, lanes]
            pieces = [(above * (1.0 + mod_ref[1:2, lanes]) + mod_ref[0:1, lanes]) * keep_p, h]
            after = win - 1 - before
            if after:
                below = xn_ref[0:after * row_len, lanes]
                pieces.append((below * (1.0 + mod_ref[1:2, lanes]) + mod_ref[0:1, lanes]) * keep_n)
            s = jnp.concatenate(pieces, axis=0)
            span = 1
            while span < win:
                keep = s.shape[0] - span * row_len
                s = s[:keep, :] + s[span * row_len:, :]
                span *= 2
            vs = s
            lo = jnp.maximum(row - before, 0)
            hi = jnp.minimum(row - before + win, n_rows)
            cnt = (hi - lo).astype(F32)
        else:
            vs = h
            cnt = jnp.ones((tm, LANES), F32)
        ahead = behind = vs
        span = 1
        while 2 * span < win:
            ahead = ahead + jnp.where(fits_after[span], shifted(ahead, span), 0.0)
            behind = behind + jnp.where(fits_before[span], shifted(behind, -span), 0.0)
            span *= 2
        hs = ahead + jnp.where(fits_before[1], shifted(behind, -1), 0.0)
        lo = jnp.maximum(col - before, 0)
        hi = jnp.minimum(col - before + win, row_len)
        cnt = cnt * (hi - lo).astype(F32)
        p = (hs / widen(cnt) - h).astype(BF16)
        ys.append(jnp.dot(p, w_ref[gi], preferred_element_type=F32))
    y = jnp.concatenate(ys, axis=1) * sc_ref[...]
    o_ref[...] = _ln_residual(x, y, mod_ref, ln_ref, alpha)


def _pool(x, mod, ln, w, scale, alpha, row_len, rows_per_tile):
    t, d = x.shape
    n_rows = t // row_len
    groups = len(POOL_WINDOWS)
    pool_g = d // groups
    halo_rows = 0 if n_rows == 1 else max(POOL_WINDOWS) // 2
    rows_per_tile = min(rows_per_tile, n_rows)
    tm = rows_per_tile * row_len
    nt = n_rows // rows_per_tile
    common = [
        pl.BlockSpec((3, d), lambda i: (0, 0)),
        pl.BlockSpec((2, d), lambda i: (0, 0)),
        pl.BlockSpec((groups, pool_g, pool_g), lambda i: (0, 0, 0)),
        pl.BlockSpec((1, d), lambda i: (0, 0)),
    ]
    if halo_rows:
        assert rows_per_tile % halo_rows == 0
        nh = halo_rows * row_len
        per = tm // nh
        last = t // nh - 1
        in_specs = [
            pl.BlockSpec((tm, d), lambda i: (i, 0)),
            pl.BlockSpec((nh, d), lambda i: (jnp.maximum(i * per - 1, 0), 0)),
            pl.BlockSpec((nh, d), lambda i: (jnp.minimum((i + 1) * per, last), 0)),
        ] + common
        args = (x, x, x, mod, ln, w, scale)
    else:
        in_specs = [pl.BlockSpec((tm, d), lambda i: (i, 0))] + common
        args = (x, mod, ln, w, scale)
    return pl.pallas_call(
        functools.partial(_pool_kernel, alpha=alpha, row_len=row_len, n_rows=n_rows,
                          rows_per_tile=rows_per_tile, halo_rows=halo_rows, pool_g=pool_g),
        grid=(nt,),
        in_specs=in_specs,
        out_specs=pl.BlockSpec((tm, d), lambda i: (i, 0)),
        out_shape=jax.ShapeDtypeStruct((t, d), F32),
        compiler_params=_params("parallel"),
        name="pool_mixer",
    )(*args)


def _proj_kernel(*refs, mode, layer, q_scale):
    if mode == "gate":
        x_ref, mod_ref, w_ref, lb_ref, k_ref, g2_ref, safe_ref = refs
    else:
        x_ref, mod_ref, w_ref, o_ref = refs
    tm = x_ref.shape[0]
    rc = min(PROJ_ROW_CHUNK, tm)
    c = SCAN_CHUNK

    if mode == "gate":
        raw = lb_ref[...]
        e = jnp.exp(raw - jnp.max(raw, axis=0, keepdims=True))
        p = e / jnp.sum(e, axis=0, keepdims=True)
        lb = jnp.sum(p[1:layer + 1, :], axis=0, keepdims=True) if layer > 0 else jnp.zeros_like(p[0:1, :])
        log_lb = jnp.log(lb)
        log_1m_lb = jnp.log(1.0 - lb)

    def matmul(r):
        rows = slice(r * rc, (r + 1) * rc)
        h = _mod_in(x_ref[rows, :], mod_ref).astype(BF16)
        return jnp.dot(h, w_ref[...], preferred_element_type=F32)

    def epilogue(r, z):
        rows = slice(r * rc, (r + 1) * rc)
        if mode == "silu":
            o_ref[rows, :] = (z * _sigmoid(z) * q_scale).astype(BF16)
        elif mode == "linear":
            o_ref[rows, :] = z.astype(BF16)
        else:
            ez = jnp.exp2(jnp.abs(z) * NEG_LOG2_E)
            one_ez = 1.0 + ez
            x2 = log_1m_lb + (jnp.minimum(z, 0.0) - jnp.log(one_ez))
            g = jnp.maximum(log_lb, x2) + jnp.log(1.0 + jnp.exp2(jnp.abs(log_lb - x2) * NEG_LOG2_E))
            inv = 1.0 / one_ez
            k = (1.0 - lb) * jnp.where(z >= 0.0, ez * inv, inv)
            gh = g.astype(BF16)
            gl = (g - gh.astype(F32)).astype(BF16)
            k_ref[rows, :] = k.astype(BF16)
            chunks = slice(r * (rc // c), (r + 1) * (rc // c))
            g2_ref[chunks, 0:c, :] = gh.reshape(rc // c, c, g.shape[1])
            g2_ref[chunks, c:2 * c, :] = gl.reshape(rc // c, c, g.shape[1])
            return jnp.min(jnp.sum(g.reshape(rc // c, c, g.shape[1]), axis=1), axis=0, keepdims=True)

    z = matmul(0)
    lowest = None
    for r in range(tm // rc):
        z_next = matmul(r + 1) if r + 1 < tm // rc else None
        low = epilogue(r, z)
        if mode == "gate":
            lowest = low if lowest is None else jnp.minimum(lowest, low)
        z = z_next

    if mode == "gate":
        width = SCAN_HEADS * HEAD_DIM
        lane = lax.broadcasted_iota(jnp.int32, safe_ref.shape[1:], 1)
        flags = jnp.zeros(safe_ref.shape[1:], jnp.int32)
        for b in range(lowest.shape[1] // width):
            ok = jnp.min(lowest[:, b * width:(b + 1) * width], axis=1, keepdims=True) >= SAFE_LOG_DECAY
            flags = jnp.where(jnp.logical_and(lane == b, ok), 1, flags)
        safe_ref[0] = flags


def _proj(x, mod, w, col, mode, tm, lb_raw=None, layer=0, q_scale=1.0):
    w, w_layer = w
    t, d = x.shape
    n = d
    tm = min(tm, t)
    in_specs = [
        pl.BlockSpec((tm, d), lambda i: (i, 0)),
        pl.BlockSpec((3, d), lambda i: (0, 0)),
        pl.BlockSpec((None, d, n), lambda i: (w_layer, 0, col)),
    ]
    out_spec = pl.BlockSpec((tm, n), lambda i: (i, 0))
    out_sds = jax.ShapeDtypeStruct((t, n), BF16)
    args = [x, mod, w]
    if mode == "gate":
        in_specs.append(pl.BlockSpec(lb_raw.shape, lambda i: (0, 0)))
        args.append(lb_raw)
        c = SCAN_CHUNK
        assert tm == min(SCAN_BLOCK, t)
        out_specs = [out_spec, pl.BlockSpec((tm // c, 2 * c, n), lambda i: (i, 0, 0)),
                     pl.BlockSpec((1, 8, LANES), lambda i: (i, 0, 0))]
        out_shape = [out_sds, jax.ShapeDtypeStruct((t // c, 2 * c, n), BF16),
                     jax.ShapeDtypeStruct((t // tm, 8, LANES), jnp.int32)]
    else:
        out_specs, out_shape = out_spec, out_sds
    return pl.pallas_call(
        functools.partial(_proj_kernel, mode=mode, layer=layer, q_scale=q_scale),
        grid=(t // tm,),
        in_specs=in_specs,
        out_specs=out_specs,
        out_shape=out_shape,
        compiler_params=_params("parallel"),
        name="hgrn_proj_" + mode,
    )(*args)


def _scan_tables(reverse):
    c = SCAN_CHUNK
    pos = np.arange(c)[::-1] if reverse else np.arange(c)
    pt, pr = pos[:, None], pos[None, :]
    blocks = [pr <= pt]
    masks = [pt == pr]
    b = c
    while b >= 2:
        mid = (pt // b) * b + b // 2
        late = pt >= mid
        if b < SCAN_SMALL_BLOCK:
            blocks.append(np.where(late, (pr >= mid) & (pr <= pt), (pr > pt) & (pr < mid)))
        mid_s = (pr // b) * b + b // 2
        masks.append((pt // b == pr // b) & late & (pr < mid_s))
        b //= 2
    w = np.concatenate(blocks, axis=0).astype(np.float32)
    w = np.concatenate([w, w], axis=1)
    m = np.stack(masks, axis=0).astype(np.float32)
    return jnp.asarray(w, BF16), jnp.asarray(m, F32)


def _dot_nt(a, b):
    return lax.dot_general(a, b, (((1,), (1,)), ((), ())), preferred_element_type=F32)


def _dot_tn(a, b):
    return lax.dot_general(a, b, (((0,), (0,)), ((), ())), preferred_element_type=F32)


def _scan_kernel(safe_ref, q_ref, k_ref, v_ref, g2_ref, s0_ref, w_ref, m_ref, o_ref, sf_ref, s_ref,
                 *, reverse, heads, n_chunks):
    j = pl.program_id(1)
    c = SCAN_CHUNK
    width = heads * HEAD_DIM
    tot_row = 0 if reverse else c - 1

    @pl.when(j == 0)
    def _():
        s_ref[...] = s0_ref[...]

    n_small = w_ref.shape[0] // c - 1
    order = range(n_chunks - 1, -1, -1) if reverse else range(n_chunks)

    def column_decay(e_row):
        return jnp.transpose(jnp.broadcast_to(e_row, (HEAD_DIM, HEAD_DIM)))

    def read_out(q_in, a, state_bf16, v):
        lhs = jnp.concatenate([q_in, a.astype(BF16)], axis=1)
        rhs = jnp.concatenate([state_bf16, v], axis=0)
        return jnp.dot(lhs, rhs, preferred_element_type=F32)

    def finish(rows, lanes, hh, q_in, a, v, k_out, e_tot):
        st = s_ref[hh]
        o_ref[rows, lanes] = read_out(q_in, a, st.astype(BF16), v).astype(o_ref.dtype)
        s_ref[hh] = st * column_decay(e_tot[:, lanes]) + _dot_tn(k_out, v)

    block = (pl.num_programs(1) - 1 - j) if reverse else j
    safe = safe_ref[block, pl.program_id(0)] == 1

    @pl.when(safe)
    def _():
        pair_mask = jnp.sum(m_ref[...], axis=0) > 0.5
        w_cum = w_ref[0:c, :]
        units = [(cc, hh) for cc in order for hh in range(heads)]
        rows_of = lambda cc: slice(cc * c, (cc + 1) * c)
        lanes_of = lambda hh: slice(hh * HEAD_DIM, (hh + 1) * HEAD_DIM)
        cums = {cc: jnp.dot(w_cum, g2_ref[cc], preferred_element_type=F32) for cc in order}
        e_tot, q_in, k_inv, k_out = {}, {}, {}, {}
        for cc in order:
            cum = cums[cc]
            e_cum = jnp.exp(cum)
            e_tot[cc] = e_cum[tot_row:tot_row + 1]
            q_in[cc] = q_ref[rows_of(cc), :] * e_cum.astype(BF16)
            k = k_ref[rows_of(cc), :]
            k_inv[cc] = k * jnp.exp(-cum).astype(BF16)
            tot = jnp.broadcast_to(cum[tot_row:tot_row + 1], (c, width))
            k_out[cc] = k * jnp.exp(tot - cum).astype(BF16)
        pair = {(cc, hh): _dot_nt(q_in[cc][:, lanes_of(hh)], k_inv[cc][:, lanes_of(hh)]) for cc, hh in units}
        upd = {(cc, hh): _dot_tn(k_out[cc][:, lanes_of(hh)], v_ref[rows_of(cc), lanes_of(hh)])
               for cc, hh in units}
        decay = {(cc, hh): column_decay(e_tot[cc][:, lanes_of(hh)]) for cc, hh in units}
        states = {}
        for hh in range(heads):
            st = s_ref[hh]
            for cc in order:
                states[cc, hh] = st.astype(BF16)
                st = st * decay[cc, hh] + upd[cc, hh]
            s_ref[hh] = st
        for cc, hh in units:
            o = read_out(q_in[cc][:, lanes_of(hh)], jnp.where(pair_mask, pair[cc, hh], 0.0),
                         states[cc, hh], v_ref[rows_of(cc), lanes_of(hh)])
            o_ref[rows_of(cc), lanes_of(hh)] = o.astype(o_ref.dtype)

    @pl.when(jnp.logical_not(safe))
    def _():
        w = w_ref[...]
        masks = [m_ref[i] > 0.5 for i in range(m_ref.shape[0])]

        @pl.loop(0, n_chunks)
        def _(ci):
            cc = (n_chunks - 1 - ci) if reverse else ci
            rows = pl.ds(pl.multiple_of(cc * c, c), c)
            sums = jnp.dot(w, g2_ref[cc], preferred_element_type=F32)
            cum = sums[0:c]
            exps = [cum, jnp.broadcast_to(cum[tot_row:tot_row + 1], (c, width)) - cum]
            b = c
            while b >= SCAN_SMALL_BLOCK:
                off = b // 2 if reverse else b // 2 - 1
                ref = jnp.concatenate(
                    [jnp.broadcast_to(cum[a + off:a + off + 1], (b, width)) for a in range(0, c, b)], axis=0)
                exps.append(-jnp.abs(cum - ref))
                b //= 2
            for i in range(n_small):
                exps.append(sums[(1 + i) * c:(2 + i) * c])
            e_f32 = jnp.exp(exps[0])
            e_tot = e_f32[tot_row:tot_row + 1]
            e_all = [e_f32.astype(BF16)] + [jnp.exp(x).astype(BF16) for x in exps[1:]]
            for hh in range(heads):
                lanes = slice(hh * HEAD_DIM, (hh + 1) * HEAD_DIM)
                q = q_ref[rows, lanes]
                k = k_ref[rows, lanes]
                a = jnp.where(masks[0], _dot_nt(q, k), 0.0)
                for lv in range(len(masks) - 1):
                    el = e_all[2 + lv][:, lanes]
                    a = jnp.where(masks[1 + lv], _dot_nt(q * el, k * el), a)
                finish(rows, lanes, hh, q * e_all[0][:, lanes], a, v_ref[rows, lanes],
                       k * e_all[1][:, lanes], e_tot)

    @pl.when(j == pl.num_programs(1) - 1)
    def _():
        sf_ref[...] = s_ref[...]


def _scan(q, k, v, g2, safe_tiles, s0, reverse):
    t, d = q.shape
    c = SCAN_CHUNK
    n_heads = d // HEAD_DIM
    hb = min(SCAN_HEADS, n_heads)
    tb = min(SCAN_BLOCK, t)
    nb = t // tb
    width = hb * HEAD_DIM
    w, m = _scan_tables(reverse)
    safe = safe_tiles[:, 0, :n_heads // hb]
    tok = (lambda h, j, s: (nb - 1 - j, h)) if reverse else (lambda h, j, s: (j, h))
    tok3 = (lambda h, j, s: (nb - 1 - j, 0, h)) if reverse else (lambda h, j, s: (j, 0, h))
    seq_spec = pl.BlockSpec((tb, width), tok)
    state_spec = pl.BlockSpec((hb, HEAD_DIM, HEAD_DIM), lambda h, j, s: (h, 0, 0))
    return pl.pallas_call(
        functools.partial(_scan_kernel, reverse=reverse, heads=hb, n_chunks=tb // c),
        grid_spec=pltpu.PrefetchScalarGridSpec(
            num_scalar_prefetch=1,
            grid=(n_heads // hb, nb),
            in_specs=[seq_spec] * 3 + [
                pl.BlockSpec((tb // c, 2 * c, width), tok3),
                state_spec,
                pl.BlockSpec(w.shape, lambda h, j, s: (0, 0)),
                pl.BlockSpec(m.shape, lambda h, j, s: (0, 0, 0)),
            ],
            out_specs=[seq_spec, state_spec],
            scratch_shapes=[pltpu.VMEM((hb, HEAD_DIM, HEAD_DIM), F32)],
        ),
        out_shape=[jax.ShapeDtypeStruct((t, d), BF16),
                   jax.ShapeDtypeStruct((n_heads, HEAD_DIM, HEAD_DIM), F32)],
        compiler_params=_params("parallel", "arbitrary"),
        name="hgrn_scan_bwd" if reverse else "hgrn_scan_fwd",
    )(safe, q, k, v, g2, s0, w, m)


def _readout_kernel(x_ref, of_ref, ob_ref, og_ref, ng_ref, w_ref, mod_ref, ln_ref, o_ref, *, alpha, n_heads):
    tm = x_ref.shape[0]
    rc = min(PROJ_ROW_CHUNK, tm)

    def matmul(r):
        rows = slice(r * rc, (r + 1) * rc)
        parts = []
        for hh in range(n_heads):
            lanes = slice(hh * HEAD_DIM, (hh + 1) * HEAD_DIM)
            o = of_ref[rows, lanes].astype(F32) + ob_ref[rows, lanes].astype(F32)
            ms = jnp.mean(o * o, axis=-1, keepdims=True)
            nrm = o * lax.rsqrt(ms + RMS_EPS) * ng_ref[...]
            parts.append((nrm * og_ref[rows, lanes].astype(F32)).astype(BF16))
        return jnp.dot(jnp.concatenate(parts, axis=1), w_ref[...], preferred_element_type=F32)

    y = matmul(0)
    for r in range(tm // rc):
        rows = slice(r * rc, (r + 1) * rc)
        y_next = matmul(r + 1) if r + 1 < tm // rc else None
        o_ref[rows, :] = _ln_residual(x_ref[rows, :], y, mod_ref, ln_ref, alpha)
        y = y_next


def _readout(x, o_f, o_b, og, norm_g, w_out, mod, ln, alpha, tm):
    t, d = x.shape
    tm = min(tm, t)
    tile = pl.BlockSpec((tm, d), lambda i: (i, 0))
    return pl.pallas_call(
        functools.partial(_readout_kernel, alpha=alpha, n_heads=d // HEAD_DIM),
        grid=(t // tm,),
        in_specs=[tile, tile, tile, tile,
                  pl.BlockSpec((1, HEAD_DIM), lambda i: (0, 0)),
                  pl.BlockSpec((d, d), lambda i: (0, 0)),
                  pl.BlockSpec((3, d), lambda i: (0, 0)),
                  pl.BlockSpec((2, d), lambda i: (0, 0))],
        out_specs=tile,
        out_shape=jax.ShapeDtypeStruct((t, d), F32),
        compiler_params=_params("parallel"),
        name="hgrn_readout",
    )(x, o_f, o_b, og, norm_g, w_out, mod, ln)


def kernel(x, c, ctx, c_ctx, mod_w, mod_b, ln_g, ln_b, ffn_w_in, ffn_w_out, pool_w, pool_scale,
           hgrn_w_in, hgrn_lb, hgrn_norm_g, hgrn_w_out):
    batch, seq, d = x.shape
    assert batch == 1 and c.shape[0] == 1
    depth = mod_w.shape[0]
    alpha = float((2 * depth) ** 0.25)
    n_sub = mod_w.shape[2] // (3 * d)
    n_heads = d // HEAD_DIM
    tm = 512

    xs = x.reshape(seq, d)
    cs = ctx.reshape(ctx.shape[1], d)
    cb = jnp.broadcast_to(jnp.stack([c.reshape(d), c_ctx.reshape(d)])[:, :, None], (2, d, LANES))
    mods = _modulation(cb, mod_w, mod_b).reshape(depth, 2, n_sub, 3, d)
    ln = jnp.stack([ln_g, ln_b], axis=2)
    w_in_b = ffn_w_in.astype(BF16)
    w_out_b = ffn_w_out.astype(BF16)
    w_hgrn_b = hgrn_w_in.astype(BF16)

    for i in range(depth):
        last = i == depth - 1
        kind = i % 2
        jm = i // 2
        ctx_used = (not last) or kind == 1
        mx, mc = mods[i, 0], mods[i, 1]

        xs = _ffn(xs, mx[0], ln[i, 0], w_in_b, w_out_b, i, 0, alpha, FFN_ROWS)
        if ctx_used:
            cs = _ffn(cs, mc[0], ln[i, 0], w_in_b, w_out_b, i, 0, alpha, tm)

        if kind == 0:
            pw = pool_w[jm].astype(BF16)
            ps = pool_scale[jm].reshape(1, d)
            xs_new = _pool(xs, mx[1], ln[i, 1], pw, ps, alpha, GRID_W, 8)
            if not last:
                cs = _pool(cs, mc[1], ln[i, 1], pw, ps, alpha, cs.shape[0], 1)
            xs = xs_new
        else:
            wh = (w_hgrn_b, jm)
            q_scale = HEAD_DIM ** -0.5
            zero = jnp.zeros((n_heads, HEAD_DIM, HEAD_DIM), F32)
            vc = _proj(cs, mc[1], wh, 1, "linear", tm)
            kfc, gfc, safe_fc = _proj(cs, mc[1], wh, 2, "gate", tm, hgrn_lb[0], i)
            kbc, gbc, safe_bc = _proj(cs, mc[1], wh, 3, "gate", tm, hgrn_lb[1], i)
            if not last:
                qc = _proj(cs, mc[1], wh, 0, "silu", tm, q_scale=q_scale)
                ogc = _proj(cs, mc[1], wh, 4, "silu", tm)
            else:
                qc = vc
            ocf, s_f = _scan(qc, kfc, vc, gfc, safe_fc, zero, False)
            ocb, s_b = _scan(qc, kbc, vc, gbc, safe_bc, zero, True)
            q = _proj(xs, mx[1], wh, 0, "silu", tm, q_scale=q_scale)
            v = _proj(xs, mx[1], wh, 1, "linear", tm)
            kf, gf, safe_f = _proj(xs, mx[1], wh, 2, "gate", tm, hgrn_lb[0], i)
            kb, gb, safe_b = _proj(xs, mx[1], wh, 3, "gate", tm, hgrn_lb[1], i)
            og = _proj(xs, mx[1], wh, 4, "silu", tm)
            o_f, _ = _scan(q, kf, v, gf, safe_f, s_f, False)
            o_b, _ = _scan(q, kb, v, gb, safe_b, s_b, True)
            ng = hgrn_norm_g[jm].reshape(1, HEAD_DIM)
            wo = hgrn_w_out[jm].astype(BF16)
            xs_new = _readout(xs, o_f, o_b, og, ng, wo, mx[1], ln[i, 1], alpha, tm)
            if not last:
                cs = _readout(cs, ocf, ocb, ogc, ng, wo, mc[1], ln[i, 1], alpha, tm)
            xs = xs_new

        xs = _ffn(xs, mx[2], ln[i, 2], w_in_b, w_out_b, i, 1, alpha, FFN_ROWS)
        if not last:
            cs = _ffn(cs, mc[2], ln[i, 2], w_in_b, w_out_b, i, 1, alpha, tm)

    return xs.reshape(batch, seq, d)
```
